```python
import math
import jax, jax.numpy as jnp
from jax import lax
import numpy as np

D_MODEL = 1024
BATCH = 32
SEQ = 2048
DEPTH = 2

N_MIXERS = 2
N_MOBA = (DEPTH + 1) // 2
N_GDN = DEPTH // 2

MOBA_HEAD_DIM = 128
MOBA_HEADS = D_MODEL // MOBA_HEAD_DIM
MOBA_WIDTH = MOBA_HEADS * MOBA_HEAD_DIM
MOBA_BLOCK = 256
MOBA_TOP_K = 3
MOBA_Q_CHUNK = 64
MOBA_IN = 4 * MOBA_WIDTH

GDN_HEAD_DIM = 128
GDN_HEADS = D_MODEL // GDN_HEAD_DIM
GDN_WIDTH = GDN_HEADS * GDN_HEAD_DIM
GDN_CONV = 4
GDN_CHUNK = 64
GDN_IN = 4 * GDN_WIDTH + 2 * GDN_HEADS

DEEPNORM_ALPHA = (2.0 * DEPTH) ** 0.25
DEEPNORM_BETA = (8.0 * DEPTH) ** -0.25
LN_EPS = 1e-5
RMS_EPS = 1e-6
L2_EPS = 1e-6
ADA_INIT_SCALE = 0.1
NEG = -1e30

kernel_name = "hybrid_moba_gdn_adaln_deepnorm"


def layer_norm(x, g, b):
    xf = x.astype(jnp.float32)
    mu = jnp.mean(xf, axis=-1, keepdims=True)
    var = jnp.mean(jnp.square(xf - mu), axis=-1, keepdims=True)
    y = (xf - mu) * lax.rsqrt(var + LN_EPS)
    return (y * g.astype(jnp.float32) + b.astype(jnp.float32)).astype(x.dtype)


def l2_normalize(x):
    return x * lax.rsqrt(jnp.sum(jnp.square(x), axis=-1, keepdims=True) + L2_EPS)


def causal_depthwise_conv(u, w):
    k_w, ch = w.shape
    return lax.conv_general_dilated(
        u, w[:, None, :], window_strides=(1,), padding=[(k_w - 1, 0)],
        dimension_numbers=('NWC', 'WIO', 'NWC'), feature_group_count=ch)


def moba_attention(q, k, v):
    B, S, H, Dh = q.shape
    L = MOBA_BLOCK
    nb = -(-S // L)
    s_pad = nb * L
    pad = [(0, 0), (0, s_pad - S), (0, 0), (0, 0)]
    qh = jnp.pad(q, pad).transpose(0, 2, 1, 3)
    kb = jnp.pad(k, pad).transpose(0, 2, 1, 3).reshape(B, H, nb, L, Dh)
    vb = jnp.pad(v, pad).transpose(0, 2, 1, 3).reshape(B, H, nb, L, Dh)

    k_mean = jnp.mean(kb.astype(jnp.float32), axis=3)
    gate = jnp.einsum('bhsd,bhnd->bhsn', qh.astype(jnp.float32), k_mean)
    q_blk = jnp.arange(s_pad) // L
    cand = jnp.arange(nb)[None, :] < q_blk[:, None]
    gate = jnp.where(cand, gate, NEG)
    n_sel = max(1, min(MOBA_TOP_K, nb - 1))
    _, sel = lax.top_k(gate, n_sel)
    sel_valid = sel < q_blk[:, None]

    qc_n = MOBA_Q_CHUNK
    nqc = s_pad // qc_n
    cpb = L // qc_n

    def to_items(a):
        rest = a.shape[3:]
        a = jnp.moveaxis(a.reshape(B, H, nqc, qc_n, *rest), 2, 1)
        return a.reshape(B * nqc, H, qc_n, *rest)

    b_ids = jnp.repeat(jnp.arange(B, dtype=jnp.int32), nqc)
    c_ids = jnp.tile(jnp.arange(nqc, dtype=jnp.int32), B)
    scale = MOBA_HEAD_DIM ** -0.5
    h_idx = jnp.arange(H)[:, None, None]

    def attend(item):
        qc, selc, validc, b, ci = item
        kbb = kb[b]
        vbb = vb[b]
        kg = kbb[h_idx, selc]
        vg = vbb[h_idx, selc]
        blk = ci // cpb
        k_own = lax.dynamic_index_in_dim(kbb, blk, axis=1, keepdims=False)
        v_own = lax.dynamic_index_in_dim(vbb, blk, axis=1, keepdims=False)
        s_sel = jnp.einsum('hqd,hqnld->hqnl', qc, kg).astype(jnp.float32) * scale
        s_sel = jnp.where(validc[..., None], s_sel, NEG)
        s_own = jnp.einsum('hqd,hld->hql', qc, k_own).astype(jnp.float32) * scale
        q_pos = (ci % cpb) * qc_n + jnp.arange(qc_n)
        causal = jnp.arange(L)[None, :] <= q_pos[:, None]
        s_own = jnp.where(causal, s_own, NEG)
        s_all = jnp.concatenate([s_sel.reshape(H, qc_n, n_sel * L), s_own], axis=-1)
        p = jax.nn.softmax(s_all, axis=-1).astype(v.dtype)
        p_sel = p[..., :n_sel * L].reshape(H, qc_n, n_sel, L)
        p_own = p[..., n_sel * L:]
        return (jnp.einsum('hqnl,hqnld->hqd', p_sel, vg)
                + jnp.einsum('hql,hld->hqd', p_own, v_own))

    out = lax.map(attend, (to_items(qh), to_items(sel), to_items(sel_valid), b_ids, c_ids))
    out = out.reshape(B, nqc, H, qc_n, Dh).transpose(0, 1, 3, 2, 4).reshape(B, s_pad, H, Dh)
    return out[:, :S]


def chunk_gated_delta_rule(q, k, v, g, beta):
    B, S, H, Dk = q.shape
    Dv = v.shape[-1]
    C = GDN_CHUNK
    n = S // C

    def blk(a):
        return jnp.moveaxis(a.reshape(B, n, C, H, *a.shape[3:]), 3, 1)

    q = blk(q * (Dk ** -0.5))
    k = blk(k)
    v = blk(v)
    g = blk(g)
    beta = blk(beta)
    g_cum = jnp.cumsum(g, axis=-1)
    idx = jnp.arange(C)
    incl = idx[:, None] >= idx[None, :]
    strict = idx[:, None] > idx[None, :]
    diff = g_cum[..., :, None] - g_cum[..., None, :]
    decay_incl = jnp.exp(jnp.where(incl, diff, NEG))
    decay_strict = jnp.where(strict, decay_incl, 0.0)

    k_beta = k * beta[..., None]
    lmat = jnp.einsum('bhncd,bhnjd->bhncj', k_beta, k) * decay_strict
    a_mat = lmat + jnp.eye(C, dtype=lmat.dtype)
    rhs = jnp.concatenate([v * beta[..., None], k_beta * jnp.exp(g_cum)[..., None]], axis=-1)
    sol = lax.linalg.triangular_solve(a_mat, rhs, left_side=True, lower=True, unit_diagonal=True)
    u = sol[..., :Dv]
    w = sol[..., Dv:]
    a_qk = jnp.einsum('bhncd,bhnjd->bhncj', q, k) * decay_incl

    xs = tuple(jnp.moveaxis(t, 2, 0) for t in (q, k, u, w, g_cum, a_qk))

    def step(state, inp):
        qi, ki, ui, wi, gi, aqk = inp
        v_new = ui - jnp.einsum('bhcd,bhde->bhce', wi, state)
        o = (jnp.einsum('bhcd,bhde->bhce', qi * jnp.exp(gi)[..., None], state)
             + jnp.einsum('bhcj,bhje->bhce', aqk, v_new))
        g_last = gi[..., -1]
        k_dec = ki * jnp.exp(g_last[..., None] - gi)[..., None]
        state = state * jnp.exp(g_last)[..., None, None] + jnp.einsum('bhcd,bhce->bhde', k_dec, v_new)
        return state, o

    s0 = jnp.zeros((B, H, Dk, Dv), jnp.float32)
    _, o = lax.scan(step, s0, xs)
    o = jnp.moveaxis(o, 0, 2)
    return jnp.moveaxis(o, 1, 3).reshape(B, S, H, Dv)


def moba_branch(h, w_in, w_out):
    B, S, _ = h.shape
    proj = h @ w_in
    q, k, v, z = jnp.split(proj, [MOBA_WIDTH, 2 * MOBA_WIDTH, 3 * MOBA_WIDTH], axis=-1)
    shp = (B, S, MOBA_HEADS, MOBA_HEAD_DIM)
    o = moba_attention(q.reshape(shp), k.reshape(shp), v.reshape(shp))
    o = o.reshape(B, S, MOBA_WIDTH) * jax.nn.silu(z)
    return o @ w_out


def gdn_branch(h, w_in, conv_w, a_log, dt_bias, norm_w, w_out):
    B, S, _ = h.shape
    W = GDN_WIDTH
    proj = h @ w_in
    qkv = proj[..., :3 * W].astype(jnp.float32)
    z = proj[..., 3 * W:4 * W].astype(jnp.float32)
    a = proj[..., 4 * W:4 * W + GDN_HEADS].astype(jnp.float32)
    bt = proj[..., 4 * W + GDN_HEADS:].astype(jnp.float32)
    qkv = jax.nn.silu(causal_depthwise_conv(qkv, conv_w.astype(jnp.float32)))
    shp = (B, S, GDN_HEADS, GDN_HEAD_DIM)
    q = l2_normalize(qkv[..., :W].reshape(shp))
    k = l2_normalize(qkv[..., W:2 * W].reshape(shp))
    v = qkv[..., 2 * W:].reshape(shp)
    g = -jnp.exp(a_log.astype(jnp.float32)) * jax.nn.softplus(a + dt_bias.astype(jnp.float32))
    beta = jax.nn.sigmoid(bt)
    o = chunk_gated_delta_rule(q, k, v, g, beta)
    o = o * lax.rsqrt(jnp.mean(jnp.square(o), axis=-1, keepdims=True) + RMS_EPS)
    o = o * norm_w.astype(jnp.float32) * jax.nn.silu(z.reshape(shp))
    return o.reshape(B, S, W).astype(h.dtype) @ w_out


def setup_inputs(seed: int = 0) -> dict:
    key = jax.random.key(seed)
    ks = jax.random.split(key, 16)
    f32 = jnp.float32
    x = jax.random.normal(ks[0], (BATCH, SEQ, D_MODEL), f32)
    c = jax.random.normal(ks[1], (BATCH, D_MODEL), f32)
    ada_w = jax.random.normal(ks[2], (DEPTH, D_MODEL, 3 * D_MODEL), f32) * (ADA_INIT_SCALE * D_MODEL ** -0.5)
    ada_base = jnp.concatenate([jnp.zeros((2 * D_MODEL,), f32), jnp.ones((D_MODEL,), f32)])
    ada_b = ada_base[None, :] + 0.01 * jax.random.normal(ks[3], (DEPTH, 3 * D_MODEL), f32)
    ln_g = 1.0 + 0.01 * jax.random.normal(ks[4], (DEPTH, D_MODEL), f32)
    ln_b = 0.01 * jax.random.normal(ks[5], (DEPTH, D_MODEL), f32)
    moba_w_in = jax.random.normal(ks[6], (N_MOBA, D_MODEL, MOBA_IN), f32) * D_MODEL ** -0.5
    moba_w_out = jax.random.normal(ks[7], (N_MOBA, MOBA_WIDTH, D_MODEL), f32) * (MOBA_WIDTH ** -0.5 * DEEPNORM_BETA)
    gdn_w_in = jax.random.normal(ks[8], (N_GDN, D_MODEL, GDN_IN), f32) * D_MODEL ** -0.5
    gdn_conv_w = jax.random.normal(ks[9], (N_GDN, GDN_CONV, 3 * GDN_WIDTH), f32) * GDN_CONV ** -0.5
    gdn_a_log = jnp.log(jax.random.uniform(ks[10], (N_GDN, GDN_HEADS), f32, 1.0, 16.0))
    dt = jnp.exp(jax.random.uniform(ks[11], (N_GDN, GDN_HEADS), f32, math.log(1e-3), math.log(1e-1)))
    gdn_dt_bias = dt + jnp.log(-jnp.expm1(-dt))
    gdn_norm_w = 1.0 + 0.01 * jax.random.normal(ks[12], (N_GDN, GDN_HEAD_DIM), f32)
    gdn_w_out = jax.random.normal(ks[13], (N_GDN, GDN_WIDTH, D_MODEL), f32) * (GDN_WIDTH ** -0.5 * DEEPNORM_BETA)
    return {"x": x, "c": c, "ada_w": ada_w, "ada_b": ada_b, "ln_g": ln_g, "ln_b": ln_b,
            "moba_w_in": moba_w_in, "moba_w_out": moba_w_out,
            "gdn_w_in": gdn_w_in, "gdn_conv_w": gdn_conv_w, "gdn_a_log": gdn_a_log,
            "gdn_dt_bias": gdn_dt_bias, "gdn_norm_w": gdn_norm_w, "gdn_w_out": gdn_w_out}


def reference(x, c, ada_w, ada_b, ln_g, ln_b, moba_w_in, moba_w_out,
              gdn_w_in, gdn_conv_w, gdn_a_log, gdn_dt_bias, gdn_norm_w, gdn_w_out):
    cs = jax.nn.silu(c)
    for i in range(DEPTH):
        mod = cs @ ada_w[i] + ada_b[i]
        shift, scale, gate = jnp.split(mod, 3, axis=-1)
        h = x * (1.0 + scale[:, None, :]) + shift[:, None, :]
        j = i // N_MIXERS
        if i % N_MIXERS == 0:
            y = moba_branch(h, moba_w_in[j], moba_w_out[j])
        else:
            y = gdn_branch(h, gdn_w_in[j], gdn_conv_w[j], gdn_a_log[j], gdn_dt_bias[j],
                           gdn_norm_w[j], gdn_w_out[j])
        x = layer_norm(DEEPNORM_ALPHA * x + gate[:, None, :] * y, ln_g[i], ln_b[i])
    return x
```

```python
import functools

import jax
import jax.numpy as jnp
from jax import lax
from jax.experimental import pallas as pl
from jax.experimental.pallas import tpu as pltpu

F32 = jnp.float32
BF16 = jnp.bfloat16

HEAD_DIM = 128
MOBA_BLOCK = 256
MOBA_TOP_K = 3
GDN_CONV = 4
GDN_CHUNK = 64
LN_EPS = 1e-5
RMS_EPS = 1e-6
L2_EPS = 1e-6
NEG = -1e30
LANE = 128
VMEM_LIMIT = 56 * 1024 * 1024


def _split3(x):
    x1 = x.astype(BF16)
    r1 = x - x1.astype(F32)
    x2 = r1.astype(BF16)
    x3 = (r1 - x2.astype(F32)).astype(BF16)
    return x1, x2, x3


def _dot(a, b):
    return jnp.dot(a, b, preferred_element_type=F32)


def _dot_nt(a, b):
    return lax.dot_general(a, b, (((1,), (1,)), ((), ())), preferred_element_type=F32)


def _dot_tn(a, b):
    return lax.dot_general(a, b, (((0,), (0,)), ((), ())), preferred_element_type=F32)


def _silu(x):
    return x * jax.nn.sigmoid(x)


def _mod_kernel(c_ref, w_ref, b_ref, o_ref):
    c = c_ref[...]
    cs = _silu(c)
    a1, a2, a3 = _split3(cs)
    b1, b2, b3 = _split3(w_ref[0])
    acc = _dot(a1, b1)
    acc += _dot(a1, b2) + _dot(a2, b1)
    acc += _dot(a1, b3) + _dot(a2, b2) + _dot(a3, b1)
    o_ref[0] = acc + b_ref[0]


def _modulation(c, ada_w, ada_b):
    depth, d, d3 = ada_w.shape
    bsz = c.shape[0]
    nt = d3 // d
    return pl.pallas_call(
        _mod_kernel,
        grid=(depth, nt),
        in_specs=[
            pl.BlockSpec((bsz, d), lambda i, j: (0, 0)),
            pl.BlockSpec((1, d, d), lambda i, j: (i, 0, j)),
            pl.BlockSpec((1, 1, d), lambda i, j: (i, 0, j)),
        ],
        out_specs=pl.BlockSpec((1, bsz, d), lambda i, j: (i, 0, j)),
        out_shape=jax.ShapeDtypeStruct((depth, bsz, d3), F32),
        compiler_params=pltpu.CompilerParams(vmem_limit_bytes=VMEM_LIMIT),
        name="adaln_mod",
    )(c, ada_w, ada_b.reshape(depth, 1, d3))


def _inproj_kernel(x_ref, scale_ref, shift_ref, w_ref, *rest, n_groups, group, with_ab):
    if with_ab:
        wab_ref, o_ref, abt_ref = rest
    else:
        (o_ref,) = rest
    x = x_ref[0]
    h = (x * (1.0 + scale_ref[0]) + shift_ref[0]).astype(BF16)
    per = group // LANE
    for n in range(n_groups):
        r = _dot(h, w_ref[:, n * group:(n + 1) * group])
        for j in range(per):
            o_ref[0, n * per + j] = r[:, j * LANE:(j + 1) * LANE].astype(BF16)
    if with_ab:
        abt_ref[0] = _dot_nt(wab_ref[...], h)


def _inproj(x, scale, shift, w_bf16, wab_t=None, *, tm=512, group=512):
    bsz, s, d = x.shape
    n = w_bf16.shape[1]
    nblk = n // LANE
    with_ab = wab_t is not None
    in_specs = [
        pl.BlockSpec((1, tm, d), lambda b, t: (b, t, 0)),
        pl.BlockSpec((1, 1, d), lambda b, t: (b, 0, 0)),
        pl.BlockSpec((1, 1, d), lambda b, t: (b, 0, 0)),
        pl.BlockSpec((d, n), lambda b, t: (0, 0)),
    ]
    out_specs = [pl.BlockSpec((1, nblk, tm, LANE), lambda b, t: (b, 0, t, 0))]
    out_shape = [jax.ShapeDtypeStruct((bsz, nblk, s, LANE), BF16)]
    args = [x, scale, shift, w_bf16]
    if with_ab:
        nab = wab_t.shape[0]
        in_specs.append(pl.BlockSpec((nab, d), lambda b, t: (0, 0)))
        out_specs.append(pl.BlockSpec((1, nab, tm), lambda b, t: (b, 0, t)))
        out_shape.append(jax.ShapeDtypeStruct((bsz, nab, s), F32))
        args.append(wab_t)
    res = pl.pallas_call(
        functools.partial(_inproj_kernel, n_groups=n // group, group=group, with_ab=with_ab),
        grid=(bsz, s // tm),
        in_specs=in_specs,
        out_specs=out_specs,
        out_shape=out_shape,
        compiler_params=pltpu.CompilerParams(
            dimension_semantics=("parallel", "parallel"), vmem_limit_bytes=VMEM_LIMIT),
        name="inproj_ab" if with_ab else "inproj",
    )(*args)
    return res if with_ab else res[0]


def _moba_kernel(q_ref, k_ref, v_ref, o_ref, *, nb, blk, top_k, scale):
    q = q_ref[0, 0]
    k = k_ref[0, 0]
    v = v_ref[0, 0]
    dh = q.shape[-1]
    k_mean = jnp.mean(k.astype(F32).reshape(nb, blk, dh), axis=1)
    m1, m2, m3 = _split3(k_mean)
    gate_t = _dot_nt(m1, q) + _dot_nt(m2, q) + _dot_nt(m3, q)
    v_t = v.astype(F32).T.astype(BF16)

    row = lax.broadcasted_iota(jnp.int32, (blk, blk), 0)
    col = lax.broadcasted_iota(jnp.int32, (blk, blk), 1)
    causal = row <= col

    for i in range(nb):
        qi = q[i * blk:(i + 1) * blk]
        nk = (i + 1) * blk
        s_t = _dot_nt(k[:nk], qi) * scale
        parts = []
        if i > top_k:
            g = [gate_t[j:j + 1, i * blk:(i + 1) * blk] for j in range(i)]
        for j in range(i):
            sj = s_t[j * blk:(j + 1) * blk]
            if i > top_k:
                rank = jnp.zeros((1, blk), F32)
                for jj in range(i):
                    if jj == j:
                        continue
                    ahead = (g[jj] >= g[j]) if jj < j else (g[jj] > g[j])
                    rank = rank + jnp.where(ahead, 1.0, 0.0)
                sj = jnp.where(rank < float(top_k), sj, NEG)
            parts.append(sj)
        parts.append(jnp.where(causal, s_t[i * blk:], NEG))
        mx = parts[0].max(axis=0, keepdims=True)
        for p_ in parts[1:]:
            mx = jnp.maximum(mx, p_.max(axis=0, keepdims=True))
        den = jnp.zeros((1, blk), F32)
        acc = jnp.zeros((dh, blk), F32)
        for j, p_ in enumerate(parts):
            e = jnp.exp(p_ - mx)
            den = den + e.sum(axis=0, keepdims=True)
            acc = acc + _dot(v_t[:, j * blk:(j + 1) * blk], e.astype(BF16))
        o_t = acc / den
        o_ref[0, 0, i * blk:(i + 1) * blk, :] = o_t.T.astype(BF16)


def _moba_attention(proj, *, heads):
    bsz, _, s, dh = proj.shape
    nb = s // MOBA_BLOCK
    kern = functools.partial(_moba_kernel, nb=nb, blk=MOBA_BLOCK, top_k=MOBA_TOP_K,
                             scale=float(dh) ** -0.5)
    blk = (1, 1, s, dh)
    return pl.pallas_call(
        kern,
        grid=(bsz, heads),
        in_specs=[
            pl.BlockSpec(blk, lambda b, h: (b, h, 0, 0)),
            pl.BlockSpec(blk, lambda b, h: (b, heads + h, 0, 0)),
            pl.BlockSpec(blk, lambda b, h: (b, 2 * heads + h, 0, 0)),
        ],
        out_specs=pl.BlockSpec(blk, lambda b, h: (b, h, 0, 0)),
        out_shape=jax.ShapeDtypeStruct((bsz, heads, s, dh), BF16),
        compiler_params=pltpu.CompilerParams(
            dimension_semantics=("parallel", "parallel"), vmem_limit_bytes=VMEM_LIMIT),
        name="moba_attn",
    )(proj, proj, proj)


def _outproj_kernel(o_ref, z_ref, x_ref, gate_ref, w_ref, g_ref, b_ref, y_ref, *, heads, alpha, gate_z):
    cols = []
    for h in range(heads):
        o = o_ref[0, h].astype(F32)
        if gate_z:
            o = o * _silu(z_ref[0, h].astype(F32))
        cols.append(o.astype(BF16))
    og = jnp.concatenate(cols, axis=-1)
    y = _dot(og, w_ref[...])
    r = alpha * x_ref[0] + gate_ref[0] * y
    mu = jnp.mean(r, axis=-1, keepdims=True)
    rc = r - mu
    var = jnp.mean(rc * rc, axis=-1, keepdims=True)
    y_ref[0] = rc * lax.rsqrt(var + LN_EPS) * g_ref[...] + b_ref[...]


def _outproj_ln(o, zsrc, z_off, x, gate, w_bf16, ln_g, ln_b, *, alpha, gate_z, tm=512):
    bsz, heads, s, dh = o.shape
    d = x.shape[-1]
    kern = functools.partial(_outproj_kernel, heads=heads, alpha=alpha, gate_z=gate_z)
    return pl.pallas_call(
        kern,
        grid=(bsz, s // tm),
        in_specs=[
            pl.BlockSpec((1, heads, tm, dh), lambda b, t: (b, 0, t, 0)),
            pl.BlockSpec((1, heads, tm, dh), lambda b, t: (b, z_off, t, 0)),
            pl.BlockSpec((1, tm, d), lambda b, t: (b, t, 0)),
            pl.BlockSpec((1, 1, d), lambda b, t: (b, 0, 0)),
            pl.BlockSpec((heads * dh, d), lambda b, t: (0, 0)),
            pl.BlockSpec((1, d), lambda b, t: (0, 0)),
            pl.BlockSpec((1, d), lambda b, t: (0, 0)),
        ],
        out_specs=pl.BlockSpec((1, tm, d), lambda b, t: (b, t, 0)),
        out_shape=jax.ShapeDtypeStruct((bsz, s, d), F32),
        compiler_params=pltpu.CompilerParams(
            dimension_semantics=("parallel", "parallel"), vmem_limit_bytes=VMEM_LIMIT),
        name="outproj_ln",
    )(o, zsrc, x, gate, w_bf16, ln_g.reshape(1, d), ln_b.reshape(1, d))


def _gdn_kernel(q_ref, k_ref, v_ref, z_ref, a_ref, b_ref, cw_ref, alog_ref, dtb_ref, nw_ref, o_ref,
                qn_s, kn_s, vn_s, gc_s, beta_s, oall_s, *, chunk, kconv):
    s, dh = q_ref.shape[2], q_ref.shape[3]
    n_chunks = s // chunk
    h = pl.program_id(1)

    t_idx = lax.broadcasted_iota(jnp.int32, (s, dh), 0)

    def conv_silu(u_ref, widx):
        u = u_ref[0, 0].astype(F32)
        acc = u * cw_ref[0, widx, kconv - 1:kconv, :]
        for sh in range(1, kconv):
            us = jnp.where(t_idx >= sh, pltpu.roll(u, sh, 0), 0.0)
            acc = acc + us * cw_ref[0, widx, kconv - 1 - sh:kconv - sh, :]
        return _silu(acc)

    def l2n(xv):
        return xv * lax.rsqrt(jnp.sum(xv * xv, axis=-1, keepdims=True) + L2_EPS)

    qn_s[...] = (l2n(conv_silu(q_ref, 0)) * (float(dh) ** -0.5)).astype(BF16)
    kn_s[...] = l2n(conv_silu(k_ref, 1))
    vn_s[...] = conv_silu(v_ref, 2)

    a_rows = a_ref[0, 0]
    a_log = jnp.full((1, chunk), alog_ref[h], F32)
    dt_bias = jnp.full((1, chunk), dtb_ref[h], F32)
    g_rows = -jnp.exp(a_log) * jax.nn.softplus(a_rows + dt_bias)
    r_i = lax.broadcasted_iota(jnp.int32, (chunk, chunk), 0)
    c_i = lax.broadcasted_iota(jnp.int32, (chunk, chunk), 1)
    triu = jnp.where(r_i <= c_i, 1.0, 0.0).astype(BF16)
    g1, g2, g3 = _split3(g_rows)
    gc_s[...] = _dot(g1, triu) + _dot(g2, triu) + _dot(g3, triu)
    beta_s[...] = jax.nn.sigmoid(b_ref[0, 0])

    eye = r_i == c_i
    incl = r_i >= c_i
    strict = r_i > c_i
    eye_f = jnp.where(eye, 1.0, 0.0)

    def to_col(row):
        return jnp.sum(jnp.where(eye, row, 0.0), axis=1, keepdims=True)

    def step(c, state):
        r0 = pl.multiple_of(c * chunk, chunk)
        qc = qn_s[pl.ds(r0, chunk), :]
        kc = kn_s[pl.ds(r0, chunk), :]
        vc = vn_s[pl.ds(r0, chunk), :]
        gc_row = gc_s[pl.ds(c, 1), :]
        gc_col = to_col(gc_row)
        beta_col = to_col(beta_s[pl.ds(c, 1), :])
        kc_b = kc.astype(BF16)
        kb = kc * beta_col
        dec = jnp.exp(jnp.where(incl, gc_col - gc_row, NEG))
        lmat = _dot_nt(kb.astype(BF16), kc_b) * jnp.where(strict, dec, 0.0)
        xm = -lmat
        tinv = eye_f + xm
        npow = 2
        while npow < chunk:
            xb = xm.astype(BF16)
            xm = _dot(xb, xb)
            tinv = tinv + _dot(tinv.astype(BF16), xm.astype(BF16))
            npow *= 2
        tb = tinv.astype(BF16)
        u = _dot(tb, (vc * beta_col).astype(BF16))
        w = _dot(tb, (kb * jnp.exp(gc_col)).astype(BF16))
        aqk = _dot_nt(qc, kc_b) * dec
        sb = state.astype(BF16)
        v_new = u - _dot(w.astype(BF16), sb)
        vnb = v_new.astype(BF16)
        o = _dot((qc.astype(F32) * jnp.exp(gc_col)).astype(BF16), sb) + _dot(aqk.astype(BF16), vnb)
        oall_s[pl.ds(r0, chunk), :] = o
        g_last = gc_row[:, chunk - 1:chunk]
        k_dec = kc * jnp.exp(g_last - gc_col)
        return state * jnp.exp(g_last) + _dot_tn(k_dec.astype(BF16), vnb)

    lax.fori_loop(0, n_chunks, step, jnp.zeros((dh, dh), F32))

    o = oall_s[...]
    o = o * lax.rsqrt(jnp.mean(o * o, axis=-1, keepdims=True) + RMS_EPS)
    o = o * nw_ref[...] * _silu(z_ref[0, 0].astype(F32))
    o_ref[0, 0] = o.astype(BF16)


def _gdn(proj, abt, conv_w3, a_log, dt_bias, norm_w, *, heads):
    bsz, _, s, dh = proj.shape
    n_chunks = s // GDN_CHUNK
    abt4 = abt.reshape(bsz, 2 * heads, n_chunks, GDN_CHUNK)
    kern = functools.partial(_gdn_kernel, chunk=GDN_CHUNK, kconv=GDN_CONV)
    blk = (1, 1, s, dh)
    rowblk = (1, 1, n_chunks, GDN_CHUNK)
    smem = pl.BlockSpec(memory_space=pltpu.SMEM)
    return pl.pallas_call(
        kern,
        grid=(bsz, heads),
        in_specs=[
            pl.BlockSpec(blk, lambda b, h: (b, h, 0, 0)),
            pl.BlockSpec(blk, lambda b, h: (b, heads + h, 0, 0)),
            pl.BlockSpec(blk, lambda b, h: (b, 2 * heads + h, 0, 0)),
            pl.BlockSpec(blk, lambda b, h: (b, 3 * heads + h, 0, 0)),
            pl.BlockSpec(rowblk, lambda b, h: (b, h, 0, 0)),
            pl.BlockSpec(rowblk, lambda b, h: (b, heads + h, 0, 0)),
            pl.BlockSpec((1, 3, GDN_CONV, dh), lambda b, h: (h, 0, 0, 0)),
            smem,
            smem,
            pl.BlockSpec((1, dh), lambda b, h: (0, 0)),
        ],
        out_specs=pl.BlockSpec(blk, lambda b, h: (b, h, 0, 0)),
        out_shape=jax.ShapeDtypeStruct((bsz, heads, s, dh), BF16),
        scratch_shapes=[
            pltpu.VMEM((s, dh), BF16),
            pltpu.VMEM((s, dh), F32),
            pltpu.VMEM((s, dh), F32),
            pltpu.VMEM((n_chunks, GDN_CHUNK), F32),
            pltpu.VMEM((n_chunks, GDN_CHUNK), F32),
            pltpu.VMEM((s, dh), F32),
        ],
        compiler_params=pltpu.CompilerParams(
            dimension_semantics=("parallel", "parallel"), vmem_limit_bytes=VMEM_LIMIT),
        name="gdn",
    )(proj, proj, proj, proj, abt4, abt4, conv_w3, a_log, dt_bias, norm_w.reshape(1, dh))


def kernel(x, c, ada_w, ada_b, ln_g, ln_b, moba_w_in, moba_w_out, gdn_w_in, gdn_conv_w, gdn_a_log,
           gdn_dt_bias, gdn_norm_w, gdn_w_out):
    bsz, s, d = x.shape
    depth = ada_w.shape[0]
    heads = d // HEAD_DIM
    width = heads * HEAD_DIM
    alpha = (2.0 * depth) ** 0.25

    mod = _modulation(c, ada_w, ada_b).reshape(depth, bsz, 3, 1, d)

    for i in range(depth):
        shift, scale, gate = mod[i, :, 0], mod[i, :, 1], mod[i, :, 2]
        j = i // 2
        if i % 2 == 0:
            proj = _inproj(x, scale, shift, moba_w_in[j].astype(BF16))
            o = _moba_attention(proj, heads=heads)
            x = _outproj_ln(o, proj, 3, x, gate, moba_w_out[j].astype(BF16), ln_g[i], ln_b[i],
                            alpha=alpha, gate_z=True)
        else:
            w_in = gdn_w_in[j]
            wab_t = w_in[:, 4 * width:].T.astype(BF16)
            proj, abt = _inproj(x, scale, shift, w_in[:, :4 * width].astype(BF16), wab_t)
            cw = gdn_conv_w[j].reshape(GDN_CONV, 3, heads, HEAD_DIM).transpose(2, 1, 0, 3)
            o = _gdn(proj, abt, cw, gdn_a_log[j], gdn_dt_bias[j], gdn_norm_w[j], heads=heads)
            x = _outproj_ln(o, o, 0, x, gate, gdn_w_out[j].astype(BF16), ln_g[i], ln_b[i],
                            alpha=alpha, gate_z=False)
    return x
```

```python
import functools

import jax
import jax.numpy as jnp
from jax import lax
from jax.experimental import pallas as pl
from jax.experimental.pallas import tpu as pltpu

F32 = jnp.float32
BF16 = jnp.bfloat16

HEAD_DIM = 128
MOBA_BLOCK = 256
MOBA_TOP_K = 3
GDN_CONV = 4
GDN_CHUNK = 64
LN_EPS = 1e-5
RMS_EPS = 1e-6
L2_EPS = 1e-6
NEG = -1e30
LANE = 128
SUBLANE_F32 = 8
SUBLANE_BF16 = 16
LOG2E = 1.4426950408889634
VMEM_LIMIT = 56 * 1024 * 1024
GDN_HEADS_PER_STEP = 4
GDN_CHUNKS_PER_ITER = 8


def _split3(x):
    x1 = x.astype(BF16)
    r1 = x - x1.astype(F32)
    x2 = r1.astype(BF16)
    x3 = (r1 - x2.astype(F32)).astype(BF16)
    return x1, x2, x3


def _dot(a, b):
    return jnp.dot(a, b, preferred_element_type=F32)


def _dot_nt(a, b):
    return lax.dot_general(a, b, (((1,), (1,)), ((), ())), preferred_element_type=F32)


def _dot_tn(a, b):
    return lax.dot_general(a, b, (((0,), (0,)), ((), ())), preferred_element_type=F32)


def _silu(x):
    return x * jax.nn.sigmoid(x)


def _mod_kernel(c_ref, w_ref, b_ref, o_ref):
    c = c_ref[...]
    cs = _silu(c)
    a1, a2, a3 = _split3(cs)
    b1, b2, b3 = _split3(w_ref[0])
    acc = _dot(a1, b1)
    acc += _dot(a1, b2) + _dot(a2, b1)
    acc += _dot(a1, b3) + _dot(a2, b2) + _dot(a3, b1)
    o_ref[0] = acc + b_ref[0]


def _modulation(c, ada_w, ada_b):
    depth, d, d3 = ada_w.shape
    bsz = c.shape[0]
    nt = d3 // d
    return pl.pallas_call(
        _mod_kernel,
        grid=(depth, nt),
        in_specs=[
            pl.BlockSpec((bsz, d), lambda i, j: (0, 0)),
            pl.BlockSpec((1, d, d), lambda i, j: (i, 0, j)),
            pl.BlockSpec((1, 1, d), lambda i, j: (i, 0, j)),
        ],
        out_specs=pl.BlockSpec((1, bsz, d), lambda i, j: (i, 0, j)),
        out_shape=jax.ShapeDtypeStruct((depth, bsz, d3), F32),
        compiler_params=pltpu.CompilerParams(vmem_limit_bytes=VMEM_LIMIT),
        name="adaln_mod",
    )(c, ada_w, ada_b.reshape(depth, 1, d3))


def _inproj_kernel(x_ref, scale_ref, shift_ref, w_ref, *rest, n_groups, group, gdn):
    if gdn:
        wab_ref, conv_ref, o_ref, abt_ref, carry_s = rest
        kconv = conv_ref.shape[0]
        halo = SUBLANE_F32
        n_conv = conv_ref.shape[1] // group

        @pl.when(pl.program_id(1) == 0)
        def _():
            carry_s[:, :halo, :] = jnp.zeros((n_conv, halo, group), F32)
    else:
        (o_ref,) = rest
        n_conv = 0
    x = x_ref[0]
    tm = x.shape[0]
    h = (x * (1.0 + scale_ref[0]) + shift_ref[0]).astype(BF16)
    per = group // LANE

    def project(n):
        return _dot(h, w_ref[:, n * group:(n + 1) * group])

    r_next = project(0)
    for n in range(n_groups):
        r = r_next
        if n + 1 < n_groups:
            r_next = project(n + 1)
        if n < n_conv:
            carry_s[n, halo:, :] = r
            cols = slice(n * group, (n + 1) * group)
            acc = carry_s[n, halo:, :] * conv_ref[kconv - 1:kconv, cols]
            for sh in range(1, kconv):
                acc = acc + carry_s[n, halo - sh:halo - sh + tm, :] * conv_ref[kconv - 1 - sh:kconv - sh, cols]
            carry_s[n, :halo, :] = carry_s[n, tm:, :]
            r = acc
        for j in range(per):
            o_ref[0, n * per + j] = r[:, j * LANE:(j + 1) * LANE].astype(BF16)
    if gdn:
        abt_ref[0] = _dot_nt(wab_ref[...], h)


def _inproj(x, scale, shift, w_bf16, wab_t=None, conv_w=None, *, tm=512, group=256):
    bsz, s, d = x.shape
    n = w_bf16.shape[1]
    nblk = n // LANE
    gdn = wab_t is not None
    in_specs = [
        pl.BlockSpec((1, tm, d), lambda b, t: (b, t, 0)),
        pl.BlockSpec((1, 1, d), lambda b, t: (b, 0, 0)),
        pl.BlockSpec((1, 1, d), lambda b, t: (b, 0, 0)),
        pl.BlockSpec((d, n), lambda b, t: (0, 0)),
    ]
    out_specs = [pl.BlockSpec((1, nblk, tm, LANE), lambda b, t: (b, 0, t, 0))]
    out_shape = [jax.ShapeDtypeStruct((bsz, nblk, s, LANE), BF16)]
    args = [x, scale, shift, w_bf16]
    scratch = []
    if gdn:
        nab = wab_t.shape[0]
        in_specs.append(pl.BlockSpec((nab, d), lambda b, t: (0, 0)))
        in_specs.append(pl.BlockSpec(conv_w.shape, lambda b, t: (0, 0)))
        out_specs.append(pl.BlockSpec((1, nab, tm), lambda b, t: (b, 0, t)))
        out_shape.append(jax.ShapeDtypeStruct((bsz, nab, s), F32))
        args += [wab_t, conv_w]
        scratch.append(pltpu.VMEM((conv_w.shape[1] // group, SUBLANE_F32 + tm, group), F32))
    res = pl.pallas_call(
        functools.partial(_inproj_kernel, n_groups=n // group, group=group, gdn=gdn),
        grid=(bsz, s // tm),
        in_specs=in_specs,
        out_specs=out_specs,
        out_shape=out_shape,
        scratch_shapes=scratch,
        compiler_params=pltpu.CompilerParams(
            dimension_semantics=("parallel", "arbitrary"), vmem_limit_bytes=VMEM_LIMIT),
        name="inproj_conv" if gdn else "inproj",
    )(*args)
    return res if gdn else res[0]


def _moba_kernel(q_ref, k_ref, v_ref, o_ref, *, nb, blk, top_k, scale):
    q = q_ref[0, 0]
    k = k_ref[0, 0]
    v = v_ref[0, 0]
    dh = q.shape[-1]
    k_mean = jnp.mean(k.astype(F32).reshape(nb, blk, dh), axis=1)
    m1, m2, m3 = _split3(k_mean)
    gate_t = _dot_nt(m1, q) + _dot_nt(m2, q) + _dot_nt(m3, q)
    v_t = jnp.concatenate([v.astype(F32).T, jnp.ones((SUBLANE_BF16, q.shape[0]), F32)], axis=0).astype(BF16)

    row = lax.broadcasted_iota(jnp.int32, (blk, blk), 0)
    col = lax.broadcasted_iota(jnp.int32, (blk, blk), 1)
    causal = row <= col
    c2 = scale * LOG2E

    def scores(i):
        return _dot_nt(k[:(i + 1) * blk], q[i * blk:(i + 1) * blk])

    def probs(i, s_t):
        parts = []
        if i > top_k:
            g = [gate_t[j:j + 1, i * blk:(i + 1) * blk] for j in range(i)]
        for j in range(i):
            sj = s_t[j * blk:(j + 1) * blk]
            if i > top_k:
                rank = jnp.zeros((1, blk), F32)
                for jj in range(i):
                    if jj == j:
                        continue
                    ahead = (g[jj] >= g[j]) if jj < j else (g[jj] > g[j])
                    rank = rank + jnp.where(ahead, 1.0, 0.0)
                sj = jnp.where(rank < float(top_k), sj, NEG)
            parts.append(sj)
        parts.append(jnp.where(causal, s_t[i * blk:], NEG))
        mx = parts[0].max(axis=0, keepdims=True)
        for p_ in parts[1:]:
            mx = jnp.maximum(mx, p_.max(axis=0, keepdims=True))
        mc = mx * c2
        return [jnp.exp2(p_ * c2 - mc).astype(BF16) for p_ in parts]

    def attend(i, es):
        acc = _dot(v_t[:, :blk], es[0])
        for j in range(1, len(es)):
            acc = acc + _dot(v_t[:, j * blk:(j + 1) * blk], es[j])
        o_t = acc[:dh] / acc[dh:dh + 1]
        o_ref[0, 0, i * blk:(i + 1) * blk, :] = o_t.T.astype(BF16)

    s_next = scores(0)
    for i in range(nb):
        s_cur = s_next
        if i + 1 < nb:
            s_next = scores(i + 1)
        attend(i, probs(i, s_cur))


def _moba_attention(proj, *, heads):
    bsz, _, s, dh = proj.shape
    nb = s // MOBA_BLOCK
    kern = functools.partial(_moba_kernel, nb=nb, blk=MOBA_BLOCK, top_k=MOBA_TOP_K,
                             scale=float(dh) ** -0.5)
    blk = (1, 1, s, dh)
    return pl.pallas_call(
        kern,
        grid=(bsz, heads),
        in_specs=[
            pl.BlockSpec(blk, lambda b, h: (b, h, 0, 0)),
            pl.BlockSpec(blk, lambda b, h: (b, heads + h, 0, 0)),
            pl.BlockSpec(blk, lambda b, h: (b, 2 * heads + h, 0, 0)),
        ],
        out_specs=pl.BlockSpec(blk, lambda b, h: (b, h, 0, 0)),
        out_shape=jax.ShapeDtypeStruct((bsz, heads, s, dh), BF16),
        compiler_params=pltpu.CompilerParams(
            dimension_semantics=("parallel", "parallel"), vmem_limit_bytes=VMEM_LIMIT),
        name="moba_attn",
    )(proj, proj, proj)


def _outproj_kernel(*refs, heads, alpha, gate_z):
    if gate_z:
        o_ref, z_ref, x_ref, gate_ref, w_ref, g_ref, b_ref, y_ref = refs
    else:
        o_ref, x_ref, gate_ref, w_ref, g_ref, b_ref, y_ref = refs
    cols = []
    for h in range(heads):
        if gate_z:
            cols.append((o_ref[0, h].astype(F32) * _silu(z_ref[0, h].astype(F32))).astype(BF16))
        else:
            cols.append(o_ref[0, h])
    og = jnp.concatenate(cols, axis=-1)
    y = _dot(og, w_ref[...])
    r = alpha * x_ref[0] + gate_ref[0] * y
    mu = jnp.mean(r, axis=-1, keepdims=True)
    rc = r - mu
    var = jnp.mean(rc * rc, axis=-1, keepdims=True)
    y_ref[0] = rc * lax.rsqrt(var + LN_EPS) * g_ref[...] + b_ref[...]


def _outproj_ln(o, x, gate, w_bf16, ln_g, ln_b, *, alpha, zsrc=None, z_off=0, tm=512):
    bsz, heads, s, dh = o.shape
    d = x.shape[-1]
    gate_z = zsrc is not None
    kern = functools.partial(_outproj_kernel, heads=heads, alpha=alpha, gate_z=gate_z)
    hblk = (1, heads, tm, dh)
    in_specs = [pl.BlockSpec(hblk, lambda b, t: (b, 0, t, 0))]
    args = [o]
    if gate_z:
        in_specs.append(pl.BlockSpec(hblk, lambda b, t: (b, z_off, t, 0)))
        args.append(zsrc)
    in_specs += [
        pl.BlockSpec((1, tm, d), lambda b, t: (b, t, 0)),
        pl.BlockSpec((1, 1, d), lambda b, t: (b, 0, 0)),
        pl.BlockSpec((heads * dh, d), lambda b, t: (0, 0)),
        pl.BlockSpec((1, d), lambda b, t: (0, 0)),
        pl.BlockSpec((1, d), lambda b, t: (0, 0)),
    ]
    args += [x, gate, w_bf16, ln_g.reshape(1, d), ln_b.reshape(1, d)]
    return pl.pallas_call(
        kern,
        grid=(bsz, s // tm),
        in_specs=in_specs,
        out_specs=pl.BlockSpec((1, tm, d), lambda b, t: (b, t, 0)),
        out_shape=jax.ShapeDtypeStruct((bsz, s, d), F32),
        compiler_params=pltpu.CompilerParams(
            dimension_semantics=("parallel", "parallel"), vmem_limit_bytes=VMEM_LIMIT),
        name="outproj_ln",
    )(*args)


def _gdn_kernel(q_ref, k_ref, v_ref, z_ref, a_ref, b_ref, alog_ref, dtb_ref, nw_ref, o_ref,
                gc_s, beta_s, alast_s, mq_s, n_s, oloc_s, *, hp, chunk, cg):
    s, dh = q_ref.shape[2], q_ref.shape[3]
    n_chunks = s // chunk
    grp = pl.program_id(1)

    r_i = lax.broadcasted_iota(jnp.int32, (chunk, chunk), 0)
    c_i = lax.broadcasted_iota(jnp.int32, (chunk, chunk), 1)
    eye = r_i == c_i
    incl = r_i >= c_i
    strict = r_i > c_i
    eye_f = jnp.where(eye, 1.0, 0.0)
    triu = jnp.where(r_i <= c_i, 1.0, 0.0).astype(BF16)

    for j in range(hp):
        head = grp * hp + j
        a_log = jnp.full((1, chunk), alog_ref[head], F32)
        dt_bias = jnp.full((1, chunk), dtb_ref[head], F32)
        g_rows = -jnp.exp(a_log) * jax.nn.softplus(a_ref[0, j] + dt_bias)
        g1, g2, g3 = _split3(g_rows)
        gc = _dot(g1, triu) + _dot(g2, triu) + _dot(g3, triu)
        gc_s[j] = gc
        beta_s[j] = jax.nn.sigmoid(b_ref[0, j])
        alast_s[j] = jnp.broadcast_to(jnp.exp(gc[:, chunk - 1:chunk]), (n_chunks, dh))

    def act(u_ref, j, r0):
        return _silu(u_ref[0, j, pl.ds(r0, chunk), :].astype(F32))

    def l2n(xv):
        return xv * lax.rsqrt(jnp.sum(xv * xv, axis=-1, keepdims=True) + L2_EPS)

    def to_col(row):
        return jnp.sum(jnp.where(eye, row, 0.0), axis=1, keepdims=True)

    def prep_body(it, carry):
        chains = [(j, it * cg + cc) for cc in range(cg) for j in range(hp)]
        rows0 = [pl.multiple_of(c * chunk, chunk) for _, c in chains]
        qs = [(l2n(act(q_ref, j, r0)) * (float(dh) ** -0.5)).astype(BF16) for (j, _), r0 in zip(chains, rows0)]
        ks = [l2n(act(k_ref, j, r0)).astype(BF16) for (j, _), r0 in zip(chains, rows0)]
        vs = [act(v_ref, j, r0) for (j, _), r0 in zip(chains, rows0)]
        gc_rows = [gc_s[j, pl.ds(c, 1), :] for j, c in chains]
        gc_cols = [to_col(r) for r in gc_rows]
        beta_cols = [to_col(beta_s[j, pl.ds(c, 1), :]) for j, c in chains]
        decs = [jnp.exp(jnp.where(incl, gcc - gcr, NEG)) for gcc, gcr in zip(gc_cols, gc_rows)]
        kqks = [_dot_nt(jnp.concatenate([k, q], axis=0), k) for q, k in zip(qs, ks)]
        xms = [-(kqk[:chunk] * (jnp.where(strict, d, 0.0) * bc)) for kqk, d, bc in zip(kqks, decs, beta_cols)]
        tinvs = [eye_f + x for x in xms]
        npow = 2
        while npow < chunk:
            xbs = [x.astype(BF16) for x in xms]
            xms = [_dot(xb, xb) for xb in xbs]
            tinvs = [t + _dot(t.astype(BF16), x.astype(BF16)) for t, x in zip(tinvs, xms)]
            npow *= 2
        rhss = [jnp.concatenate([v * bc, k.astype(F32) * (-bc * jnp.exp(gcc))], axis=1).astype(BF16)
                for v, k, bc, gcc in zip(vs, ks, beta_cols, gc_cols)]
        uws = [_dot(t.astype(BF16), r).astype(BF16) for t, r in zip(tinvs, rhss)]
        aqks = [(kqk[chunk:] * d).astype(BF16) for kqk, d in zip(kqks, decs)]
        a_uws = [_dot(a, uw) for a, uw in zip(aqks, uws)]
        kds = [(k.astype(F32) * jnp.exp(gr[:, chunk - 1:chunk] - gcc)).astype(BF16)
               for k, gr, gcc in zip(ks, gc_rows, gc_cols)]
        k_uws = [_dot_tn(kd, uw) for kd, uw in zip(kds, uws)]
        for idx, (j, c) in enumerate(chains):
            qp = qs[idx].astype(F32) * jnp.exp(gc_cols[idx]) + a_uws[idx][:, dh:]
            mq_s[j, c, :dh, :] = k_uws[idx][:, dh:].astype(BF16)
            mq_s[j, c, dh:, :] = qp.astype(BF16)
            n_s[j, c] = k_uws[idx][:, :dh]
            oloc_s[j, pl.ds(rows0[idx], chunk), :] = a_uws[idx][:, :dh]
        return carry

    lax.fori_loop(0, n_chunks // cg, prep_body, 0)

    def finalize(j, c):
        r0 = pl.multiple_of(c * chunk, chunk)
        o = oloc_s[j, pl.ds(r0, chunk), :]
        o = o * lax.rsqrt(jnp.mean(o * o, axis=-1, keepdims=True) + RMS_EPS)
        o = o * nw_ref[...] * _silu(z_ref[0, j, pl.ds(r0, chunk), :].astype(F32))
        o_ref[0, j, pl.ds(r0, chunk), :] = o.astype(BF16)

    def scan_body(c, states):
        r0 = pl.multiple_of(c * chunk, chunk)
        for j in range(hp):
            finalize(j, jnp.maximum(c - 1, 0))
        new = []
        for j in range(hp):
            res = _dot(mq_s[j, c], states[j].astype(BF16))
            new.append(states[j] * alast_s[j, pl.ds(c, 1), :] + res[:dh] + n_s[j, c])
            oloc_s[j, pl.ds(r0, chunk), :] += res[dh:]
        return tuple(new)

    lax.fori_loop(0, n_chunks, scan_body, tuple(jnp.zeros((dh, dh), F32) for _ in range(hp)))
    for j in range(hp):
        finalize(j, n_chunks - 1)


def _gdn(proj, abt, a_log, dt_bias, norm_w, *, heads):
    bsz, _, s, dh = proj.shape
    hp, cg, chunk = GDN_HEADS_PER_STEP, GDN_CHUNKS_PER_ITER, GDN_CHUNK
    n_chunks = s // chunk
    ngrp = heads // hp
    abt4 = abt.reshape(bsz, 2 * heads, n_chunks, chunk)
    kern = functools.partial(_gdn_kernel, hp=hp, chunk=chunk, cg=cg)
    blk = (1, hp, s, dh)
    rowblk = (1, hp, n_chunks, chunk)
    smem = pl.BlockSpec(memory_space=pltpu.SMEM)
    return pl.pallas_call(
        kern,
        grid=(bsz, ngrp),
        in_specs=[
            pl.BlockSpec(blk, lambda b, g: (b, g, 0, 0)),
            pl.BlockSpec(blk, lambda b, g: (b, ngrp + g, 0, 0)),
            pl.BlockSpec(blk, lambda b, g: (b, 2 * ngrp + g, 0, 0)),
            pl.BlockSpec(blk, lambda b, g: (b, 3 * ngrp + g, 0, 0)),
            pl.BlockSpec(rowblk, lambda b, g: (b, g, 0, 0)),
            pl.BlockSpec(rowblk, lambda b, g: (b, ngrp + g, 0, 0)),
            smem,
            smem,
            pl.BlockSpec((1, dh), lambda b, g: (0, 0)),
        ],
        out_specs=pl.BlockSpec(blk, lambda b, g: (b, g, 0, 0)),
        out_shape=jax.ShapeDtypeStruct((bsz, heads, s, dh), BF16),
        scratch_shapes=[
            pltpu.VMEM((hp, n_chunks, chunk), F32),
            pltpu.VMEM((hp, n_chunks, chunk), F32),
            pltpu.VMEM((hp, n_chunks, dh), F32),
            pltpu.VMEM((hp, n_chunks, dh + chunk, dh), BF16),
            pltpu.VMEM((hp, n_chunks, dh, dh), F32),
            pltpu.VMEM((hp, s, dh), F32),
        ],
        compiler_params=pltpu.CompilerParams(
            dimension_semantics=("parallel", "parallel"), vmem_limit_bytes=VMEM_LIMIT),
        name="gdn",
    )(proj, proj, proj, proj, abt4, abt4, a_log, dt_bias, norm_w.reshape(1, dh))


def kernel(x, c, ada_w, ada_b, ln_g, ln_b, moba_w_in, moba_w_out, gdn_w_in, gdn_conv_w, gdn_a_log,
           gdn_dt_bias, gdn_norm_w, gdn_w_out):
    bsz, s, d = x.shape
    depth = ada_w.shape[0]
    heads = d // HEAD_DIM
    width = heads * HEAD_DIM
    alpha = (2.0 * depth) ** 0.25

    mod = _modulation(c, ada_w, ada_b).reshape(depth, bsz, 3, 1, d)

    for i in range(depth):
        shift, scale, gate = mod[i, :, 0], mod[i, :, 1], mod[i, :, 2]
        j = i // 2
        if i % 2 == 0:
            proj = _inproj(x, scale, shift, moba_w_in[j].astype(BF16))
            o = _moba_attention(proj, heads=heads)
            x = _outproj_ln(o, x, gate, moba_w_out[j].astype(BF16), ln_g[i], ln_b[i],
                            alpha=alpha, zsrc=proj, z_off=3)
        else:
            w_in = gdn_w_in[j]
            wab_t = w_in[:, 4 * width:].T.astype(BF16)
            proj, abt = _inproj(x, scale, shift, w_in[:, :4 * width].astype(BF16), wab_t, gdn_conv_w[j])
            o = _gdn(proj, abt, gdn_a_log[j], gdn_dt_bias[j], gdn_norm_w[j], heads=heads)
            x = _outproj_ln(o, x, gate, gdn_w_out[j].astype(BF16), ln_g[i], ln_b[i], alpha=alpha)
    return x
```

```python
import functools

import jax
import jax.numpy as jnp
from jax import lax
from jax.experimental import pallas as pl
from jax.experimental.pallas import tpu as pltpu

F32 = jnp.float32
BF16 = jnp.bfloat16

HEAD_DIM = 128
MOBA_BLOCK = 256
MOBA_TOP_K = 3
GDN_CONV = 4
GDN_CHUNK = 64
LN_EPS = 1e-5
RMS_EPS = 1e-6
L2_EPS = 1e-6
NEG = -1e30
LANE = 128
SUBLANE_F32 = 8
SUBLANE_BF16 = 16
LOG2E = 1.4426950408889634
VMEM_LIMIT = 56 * 1024 * 1024
MOBA_HEADS_PER_STEP = 2
GDN_HEADS_PER_STEP = 4
GDN_CHUNKS_PER_ITER = 8


def _split3(x):
    x1 = x.astype(BF16)
    r1 = x - x1.astype(F32)
    x2 = r1.astype(BF16)
    x3 = (r1 - x2.astype(F32)).astype(BF16)
    return x1, x2, x3


def _dot(a, b):
    return jnp.dot(a, b, preferred_element_type=F32)


def _dot_nt(a, b):
    return lax.dot_general(a, b, (((1,), (1,)), ((), ())), preferred_element_type=F32)


def _dot_tn(a, b):
    return lax.dot_general(a, b, (((0,), (0,)), ((), ())), preferred_element_type=F32)


def _silu(x):
    return x * jax.nn.sigmoid(x)


def _mod_kernel(c_ref, w_ref, b_ref, o_ref):
    c = c_ref[...]
    cs = _silu(c)
    a1, a2, a3 = _split3(cs)
    b1, b2, b3 = _split3(w_ref[0])
    acc = _dot(a1, b1)
    acc += _dot(a1, b2) + _dot(a2, b1)
    acc += _dot(a1, b3) + _dot(a2, b2) + _dot(a3, b1)
    o_ref[0] = acc + b_ref[0]


def _modulation(c, ada_w, ada_b):
    depth, d, d3 = ada_w.shape
    bsz = c.shape[0]
    nt = d3 // d
    return pl.pallas_call(
        _mod_kernel,
        grid=(depth, nt),
        in_specs=[
            pl.BlockSpec((bsz, d), lambda i, j: (0, 0)),
            pl.BlockSpec((1, d, d), lambda i, j: (i, 0, j)),
            pl.BlockSpec((1, 1, d), lambda i, j: (i, 0, j)),
        ],
        out_specs=pl.BlockSpec((1, bsz, d), lambda i, j: (i, 0, j)),
        out_shape=jax.ShapeDtypeStruct((depth, bsz, d3), F32),
        compiler_params=pltpu.CompilerParams(vmem_limit_bytes=VMEM_LIMIT),
        name="adaln_mod",
    )(c, ada_w, ada_b.reshape(depth, 1, d3))


def _gated_outproj_ln(o_ref, z_ref, x, gate, w_ref, g_ref, b_ref, *, heads, alpha):
    cols = []
    for h in range(heads):
        if z_ref is not None:
            cols.append((o_ref[0, h].astype(F32) * _silu(z_ref[0, h].astype(F32))).astype(BF16))
        else:
            cols.append(o_ref[0, h])
    y = _dot(jnp.concatenate(cols, axis=-1), w_ref[...])
    r = alpha * x + gate * y
    mu = jnp.mean(r, axis=-1, keepdims=True)
    rc = r - mu
    var = jnp.mean(rc * rc, axis=-1, keepdims=True)
    return rc * lax.rsqrt(var + LN_EPS) * g_ref[...] + b_ref[...]


def _inproj_kernel(*refs, n_groups, group, gdn, fused, prev_gate_z, heads, alpha):
    refs = list(refs)
    if fused:
        po_ref = refs.pop(0)
        pz_ref = refs.pop(0) if prev_gate_z else None
        x_ref, pgate_ref, pw_ref, pg_ref, pb_ref = refs[:5]
        refs = refs[5:]
    else:
        x_ref = refs.pop(0)
    scale_ref, shift_ref, w_ref = refs[:3]
    refs = refs[3:]
    if gdn:
        wab_ref, conv_ref = refs[:2]
        refs = refs[2:]
    if fused:
        x1_ref = refs.pop(0)
    o_ref = refs.pop(0)
    n_conv = 0
    if gdn:
        abt_ref, carry_s = refs
        kconv = conv_ref.shape[0]
        halo = SUBLANE_F32
        n_conv = conv_ref.shape[1] // group

        @pl.when(pl.program_id(1) == 0)
        def _():
            carry_s[:, :halo, :] = jnp.zeros((n_conv, halo, group), F32)

    x = x_ref[0]
    if fused:
        x = _gated_outproj_ln(po_ref, pz_ref, x, pgate_ref[0], pw_ref, pg_ref, pb_ref, heads=heads, alpha=alpha)
        x1_ref[0] = x
    tm = x.shape[0]
    h = (x * (1.0 + scale_ref[0]) + shift_ref[0]).astype(BF16)
    per = group // LANE
    for n in range(n_groups):
        r = _dot(h, w_ref[:, n * group:(n + 1) * group])
        if n < n_conv:
            carry_s[n, halo:, :] = r
            cols = slice(n * group, (n + 1) * group)
            acc = r * conv_ref[kconv - 1:kconv, cols]
            for sh in range(1, kconv):
                acc = acc + carry_s[n, halo - sh:halo - sh + tm, :] * conv_ref[kconv - 1 - sh:kconv - sh, cols]
            carry_s[n, :halo, :] = r[tm - halo:]
            r = acc
        for j in range(per):
            o_ref[0, n * per + j] = r[:, j * LANE:(j + 1) * LANE].astype(BF16)
    if gdn:
        abt_ref[0] = _dot_nt(wab_ref[...], h)


def _inproj(x, scale, shift, w_bf16, wab_t=None, conv_w=None, prev=None, *, alpha=None, tm=512, group=512):
    bsz, s, d = x.shape
    n = w_bf16.shape[1]
    nblk = n // LANE
    gdn = wab_t is not None
    fused = prev is not None
    const = lambda b, t: (0, 0)
    row3 = pl.BlockSpec((1, 1, d), lambda b, t: (b, 0, 0))
    in_specs, args = [], []
    heads = None
    if fused:
        o, zsrc, z_off, gate, w_out, ln_g, ln_b = prev
        heads, dh = o.shape[1], o.shape[3]
        hblk = (1, heads, tm, dh)
        in_specs.append(pl.BlockSpec(hblk, lambda b, t: (b, 0, t, 0)))
        args.append(o)
        if zsrc is not None:
            in_specs.append(pl.BlockSpec(hblk, lambda b, t: (b, z_off, t, 0)))
            args.append(zsrc)
    in_specs.append(pl.BlockSpec((1, tm, d), lambda b, t: (b, t, 0)))
    args.append(x)
    if fused:
        in_specs += [row3, pl.BlockSpec((heads * dh, d), const), pl.BlockSpec((1, d), const),
                     pl.BlockSpec((1, d), const)]
        args += [gate, w_out, ln_g.reshape(1, d), ln_b.reshape(1, d)]
    in_specs += [row3, row3, pl.BlockSpec((d, n), const)]
    args += [scale, shift, w_bf16]
    out_specs, out_shape, scratch = [], [], []
    if gdn:
        nab = wab_t.shape[0]
        in_specs += [pl.BlockSpec((nab, d), const), pl.BlockSpec(conv_w.shape, const)]
        args += [wab_t, conv_w]
    if fused:
        out_specs.append(pl.BlockSpec((1, tm, d), lambda b, t: (b, t, 0)))
        out_shape.append(jax.ShapeDtypeStruct((bsz, s, d), F32))
    out_specs.append(pl.BlockSpec((1, nblk, tm, LANE), lambda b, t: (b, 0, t, 0)))
    out_shape.append(jax.ShapeDtypeStruct((bsz, nblk, s, LANE), BF16))
    if gdn:
        out_specs.append(pl.BlockSpec((1, nab, tm), lambda b, t: (b, 0, t)))
        out_shape.append(jax.ShapeDtypeStruct((bsz, nab, s), F32))
        scratch.append(pltpu.VMEM((conv_w.shape[1] // group, SUBLANE_F32 + tm, group), F32))
    return list(pl.pallas_call(
        functools.partial(_inproj_kernel, n_groups=n // group, group=group, gdn=gdn, fused=fused,
                          prev_gate_z=fused and prev[1] is not None, heads=heads, alpha=alpha),
        grid=(bsz, s // tm),
        in_specs=in_specs,
        out_specs=out_specs,
        out_shape=out_shape,
        scratch_shapes=scratch,
        compiler_params=pltpu.CompilerParams(
            dimension_semantics=("parallel", "arbitrary"), vmem_limit_bytes=VMEM_LIMIT),
        name=("outproj_" if fused else "") + ("inproj_conv" if gdn else "inproj"),
    )(*args))


def _moba_kernel(q_ref, k_ref, v_ref, o_ref, *, hp, nb, blk, top_k, scale):
    s, dh = q_ref.shape[2], q_ref.shape[3]
    row = lax.broadcasted_iota(jnp.int32, (blk, blk), 0)
    col = lax.broadcasted_iota(jnp.int32, (blk, blk), 1)
    causal = row <= col
    c2 = scale * LOG2E

    qs, ks, vts, gates = [], [], [], []
    for h in range(hp):
        q = q_ref[0, h]
        k = k_ref[0, h]
        k_mean = jnp.mean(k.astype(F32).reshape(nb, blk, dh), axis=1)
        m1, m2, m3 = _split3(k_mean)
        gates.append(_dot_nt(m1, q) + _dot_nt(m2, q) + _dot_nt(m3, q))
        qs.append((q.astype(F32) * c2).astype(BF16))
        ks.append(k)
        vts.append(jnp.concatenate([v_ref[0, h].astype(F32).T, jnp.ones((SUBLANE_BF16, s), F32)],
                                   axis=0).astype(BF16))

    def scores(h, i):
        return _dot_nt(ks[h][:(i + 1) * blk], qs[h][i * blk:(i + 1) * blk])

    def probs(h, i, s_t):
        parts = []
        if i > top_k:
            g = [gates[h][j:j + 1, i * blk:(i + 1) * blk] for j in range(i)]
        for j in range(i):
            sj = s_t[j * blk:(j + 1) * blk]
            if i > top_k:
                rank = jnp.zeros((1, blk), F32)
                for jj in range(i):
                    if jj == j:
                        continue
                    ahead = (g[jj] >= g[j]) if jj < j else (g[jj] > g[j])
                    rank = rank + jnp.where(ahead, 1.0, 0.0)
                sj = jnp.where(rank < float(top_k), sj, NEG)
            parts.append(sj)
        parts.append(jnp.where(causal, s_t[i * blk:], NEG))
        mx = parts[0].max(axis=0, keepdims=True)
        for p_ in parts[1:]:
            mx = jnp.maximum(mx, p_.max(axis=0, keepdims=True))
        return [jnp.exp2(p_ - mx).astype(BF16) for p_ in parts]

    def attend(h, i, es):
        acc = _dot(vts[h][:, :blk], es[0])
        for j in range(1, len(es)):
            acc = acc + _dot(vts[h][:, j * blk:(j + 1) * blk], es[j])
        o_t = acc[:dh] / acc[dh:dh + 1]
        o_ref[0, h, i * blk:(i + 1) * blk, :] = o_t.T.astype(BF16)

    s_next = [scores(h, 0) for h in range(hp)]
    for i in range(nb):
        s_cur = s_next
        if i + 1 < nb:
            s_next = [scores(h, i + 1) for h in range(hp)]
        es = [probs(h, i, s_cur[h]) for h in range(hp)]
        for h in range(hp):
            attend(h, i, es[h])


def _moba_attention(proj, *, heads):
    bsz, _, s, dh = proj.shape
    nb = s // MOBA_BLOCK
    hp = MOBA_HEADS_PER_STEP
    ngrp = heads // hp
    kern = functools.partial(_moba_kernel, hp=hp, nb=nb, blk=MOBA_BLOCK, top_k=MOBA_TOP_K,
                             scale=float(dh) ** -0.5)
    blk = (1, hp, s, dh)
    return pl.pallas_call(
        kern,
        grid=(bsz, ngrp),
        in_specs=[
            pl.BlockSpec(blk, lambda b, g: (b, g, 0, 0)),
            pl.BlockSpec(blk, lambda b, g: (b, ngrp + g, 0, 0)),
            pl.BlockSpec(blk, lambda b, g: (b, 2 * ngrp + g, 0, 0)),
        ],
        out_specs=pl.BlockSpec(blk, lambda b, g: (b, g, 0, 0)),
        out_shape=jax.ShapeDtypeStruct((bsz, heads, s, dh), BF16),
        compiler_params=pltpu.CompilerParams(
            dimension_semantics=("parallel", "parallel"), vmem_limit_bytes=VMEM_LIMIT),
        name="moba_attn",
    )(proj, proj, proj)


def _outproj_kernel(*refs, heads, alpha, gate_z):
    if gate_z:
        o_ref, z_ref, x_ref, gate_ref, w_ref, g_ref, b_ref, y_ref = refs
    else:
        o_ref, x_ref, gate_ref, w_ref, g_ref, b_ref, y_ref = refs
        z_ref = None
    y_ref[0] = _gated_outproj_ln(o_ref, z_ref, x_ref[0], gate_ref[0], w_ref, g_ref, b_ref,
                                 heads=heads, alpha=alpha)


def _outproj_ln(o, x, gate, w_bf16, ln_g, ln_b, *, alpha, zsrc=None, z_off=0, tm=512):
    bsz, heads, s, dh = o.shape
    d = x.shape[-1]
    gate_z = zsrc is not None
    kern = functools.partial(_outproj_kernel, heads=heads, alpha=alpha, gate_z=gate_z)
    hblk = (1, heads, tm, dh)
    in_specs = [pl.BlockSpec(hblk, lambda b, t: (b, 0, t, 0))]
    args = [o]
    if gate_z:
        in_specs.append(pl.BlockSpec(hblk, lambda b, t: (b, z_off, t, 0)))
        args.append(zsrc)
    in_specs += [
        pl.BlockSpec((1, tm, d), lambda b, t: (b, t, 0)),
        pl.BlockSpec((1, 1, d), lambda b, t: (b, 0, 0)),
        pl.BlockSpec((heads * dh, d), lambda b, t: (0, 0)),
        pl.BlockSpec((1, d), lambda b, t: (0, 0)),
        pl.BlockSpec((1, d), lambda b, t: (0, 0)),
    ]
    args += [x, gate, w_bf16, ln_g.reshape(1, d), ln_b.reshape(1, d)]
    return pl.pallas_call(
        kern,
        grid=(bsz, s // tm),
        in_specs=in_specs,
        out_specs=pl.BlockSpec((1, tm, d), lambda b, t: (b, t, 0)),
        out_shape=jax.ShapeDtypeStruct((bsz, s, d), F32),
        compiler_params=pltpu.CompilerParams(
            dimension_semantics=("parallel", "parallel"), vmem_limit_bytes=VMEM_LIMIT),
        name="outproj_ln",
    )(*args)


def _gdn_kernel(q_ref, k_ref, v_ref, z_ref, a_ref, b_ref, alog_ref, dtb_ref, nw_ref, o_ref,
                gc_s, beta_s, alast_s, mq_s, n_s, oloc_s, *, hp, chunk, cg):
    s, dh = q_ref.shape[2], q_ref.shape[3]
    n_chunks = s // chunk
    grp = pl.program_id(1)

    r_i = lax.broadcasted_iota(jnp.int32, (chunk, chunk), 0)
    c_i = lax.broadcasted_iota(jnp.int32, (chunk, chunk), 1)
    eye = r_i == c_i
    incl = r_i >= c_i
    strict = r_i > c_i
    eye_f = jnp.where(eye, 1.0, 0.0)
    triu = jnp.where(r_i <= c_i, 1.0, 0.0).astype(BF16)

    for j in range(hp):
        head = grp * hp + j
        a_log = jnp.full((1, chunk), alog_ref[head], F32)
        dt_bias = jnp.full((1, chunk), dtb_ref[head], F32)
        g_rows = -jnp.exp(a_log) * jax.nn.softplus(a_ref[0, j] + dt_bias)
        g1, g2, g3 = _split3(g_rows)
        gc = _dot(g1, triu) + _dot(g2, triu) + _dot(g3, triu)
        gc_s[j] = gc
        beta_s[j] = jax.nn.sigmoid(b_ref[0, j])
        alast_s[j] = jnp.broadcast_to(jnp.exp(gc[:, chunk - 1:chunk]), (n_chunks, dh))

    def act(u_ref, j, r0):
        return _silu(u_ref[0, j, pl.ds(r0, chunk), :].astype(F32))

    def l2n(xv):
        return xv * lax.rsqrt(jnp.sum(xv * xv, axis=-1, keepdims=True) + L2_EPS)

    def to_col(row):
        return jnp.sum(jnp.where(eye, row, 0.0), axis=1, keepdims=True)

    def prep_body(it, carry):
        chains = [(j, it * cg + cc) for cc in range(cg) for j in range(hp)]
        rows0 = [pl.multiple_of(c * chunk, chunk) for _, c in chains]
        qs = [(l2n(act(q_ref, j, r0)) * (float(dh) ** -0.5)).astype(BF16) for (j, _), r0 in zip(chains, rows0)]
        ks = [l2n(act(k_ref, j, r0)).astype(BF16) for (j, _), r0 in zip(chains, rows0)]
        vs = [act(v_ref, j, r0) for (j, _), r0 in zip(chains, rows0)]
        gc_rows = [gc_s[j, pl.ds(c, 1), :] for j, c in chains]
        gc_cols = [to_col(r) for r in gc_rows]
        beta_cols = [to_col(beta_s[j, pl.ds(c, 1), :]) for j, c in chains]
        decs = [jnp.exp(jnp.where(incl, gcc - gcr, NEG)) for gcc, gcr in zip(gc_cols, gc_rows)]
        kqks = [_dot_nt(jnp.concatenate([k, q], axis=0), k) for q, k in zip(qs, ks)]
        xms = [-(kqk[:chunk] * (jnp.where(strict, d, 0.0) * bc)) for kqk, d, bc in zip(kqks, decs, beta_cols)]
        tinvs = [eye_f + x for x in xms]
        npow = 2
        while npow < chunk:
            xbs = [x.astype(BF16) for x in xms]
            xms = [_dot(xb, xb) for xb in xbs]
            tinvs = [t + _dot(t.astype(BF16), x.astype(BF16)) for t, x in zip(tinvs, xms)]
            npow *= 2
        rhss = [jnp.concatenate([v * bc, k.astype(F32) * (-bc * jnp.exp(gcc))], axis=1).astype(BF16)
                for v, k, bc, gcc in zip(vs, ks, beta_cols, gc_cols)]
        uws = [_dot(t.astype(BF16), r).astype(BF16) for t, r in zip(tinvs, rhss)]
        aqks = [(kqk[chunk:] * d).astype(BF16) for kqk, d in zip(kqks, decs)]
        a_uws = [_dot(a, uw) for a, uw in zip(aqks, uws)]
        kds = [(k.astype(F32) * jnp.exp(gr[:, chunk - 1:chunk] - gcc)).astype(BF16)
               for k, gr, gcc in zip(ks, gc_rows, gc_cols)]
        k_uws = [_dot_tn(kd, uw) for kd, uw in zip(kds, uws)]
        for idx, (j, c) in enumerate(chains):
            qp = qs[idx].astype(F32) * jnp.exp(gc_cols[idx]) + a_uws[idx][:, dh:]
            mq_s[j, c, :dh, :] = k_uws[idx][:, dh:].astype(BF16)
            mq_s[j, c, dh:, :] = qp.astype(BF16)
            n_s[j, c] = k_uws[idx][:, :dh]
            oloc_s[j, pl.ds(rows0[idx], chunk), :] = a_uws[idx][:, :dh]
        return carry

    lax.fori_loop(0, n_chunks // cg, prep_body, 0)

    def finalize(j, c):
        r0 = pl.multiple_of(c * chunk, chunk)
        o = oloc_s[j, pl.ds(r0, chunk), :]
        o = o * lax.rsqrt(jnp.mean(o * o, axis=-1, keepdims=True) + RMS_EPS)
        o = o * nw_ref[...] * _silu(z_ref[0, j, pl.ds(r0, chunk), :].astype(F32))
        o_ref[0, j, pl.ds(r0, chunk), :] = o.astype(BF16)

    def scan_body(c, states):
        r0 = pl.multiple_of(c * chunk, chunk)
        for j in range(hp):
            finalize(j, jnp.maximum(c - 1, 0))
        new = []
        for j in range(hp):
            res = _dot(mq_s[j, c], states[j].astype(BF16))
            new.append(states[j] * alast_s[j, pl.ds(c, 1), :] + res[:dh] + n_s[j, c])
            oloc_s[j, pl.ds(r0, chunk), :] += res[dh:]
        return tuple(new)

    lax.fori_loop(0, n_chunks, scan_body, tuple(jnp.zeros((dh, dh), F32) for _ in range(hp)))
    for j in range(hp):
        finalize(j, n_chunks - 1)


def _gdn(proj, abt, a_log, dt_bias, norm_w, *, heads):
    bsz, _, s, dh = proj.shape
    hp, cg, chunk = GDN_HEADS_PER_STEP, GDN_CHUNKS_PER_ITER, GDN_CHUNK
    n_chunks = s // chunk
    ngrp = heads // hp
    abt4 = abt.reshape(bsz, 2 * heads, n_chunks, chunk)
    kern = functools.partial(_gdn_kernel, hp=hp, chunk=chunk, cg=cg)
    blk = (1, hp, s, dh)
    rowblk = (1, hp, n_chunks, chunk)
    smem = pl.BlockSpec(memory_space=pltpu.SMEM)
    return pl.pallas_call(
        kern,
        grid=(bsz, ngrp),
        in_specs=[
            pl.BlockSpec(blk, lambda b, g: (b, g, 0, 0)),
            pl.BlockSpec(blk, lambda b, g: (b, ngrp + g, 0, 0)),
            pl.BlockSpec(blk, lambda b, g: (b, 2 * ngrp + g, 0, 0)),
            pl.BlockSpec(blk, lambda b, g: (b, 3 * ngrp + g, 0, 0)),
            pl.BlockSpec(rowblk, lambda b, g: (b, g, 0, 0)),
            pl.BlockSpec(rowblk, lambda b, g: (b, ngrp + g, 0, 0)),
            smem,
            smem,
            pl.BlockSpec((1, dh), lambda b, g: (0, 0)),
        ],
        out_specs=pl.BlockSpec(blk, lambda b, g: (b, g, 0, 0)),
        out_shape=jax.ShapeDtypeStruct((bsz, heads, s, dh), BF16),
        scratch_shapes=[
            pltpu.VMEM((hp, n_chunks, chunk), F32),
            pltpu.VMEM((hp, n_chunks, chunk), F32),
            pltpu.VMEM((hp, n_chunks, dh), F32),
            pltpu.VMEM((hp, n_chunks, dh + chunk, dh), BF16),
            pltpu.VMEM((hp, n_chunks, dh, dh), F32),
            pltpu.VMEM((hp, s, dh), F32),
        ],
        compiler_params=pltpu.CompilerParams(
            dimension_semantics=("parallel", "parallel"), vmem_limit_bytes=VMEM_LIMIT),
        name="gdn",
    )(proj, proj, proj, proj, abt4, abt4, a_log, dt_bias, norm_w.reshape(1, dh))


def kernel(x, c, ada_w, ada_b, ln_g, ln_b, moba_w_in, moba_w_out, gdn_w_in, gdn_conv_w, gdn_a_log,
           gdn_dt_bias, gdn_norm_w, gdn_w_out):
    bsz, s, d = x.shape
    depth = ada_w.shape[0]
    heads = d // HEAD_DIM
    width = heads * HEAD_DIM
    alpha = (2.0 * depth) ** 0.25

    mod = _modulation(c, ada_w, ada_b).reshape(depth, bsz, 3, 1, d)

    pending = None
    for i in range(depth):
        shift, scale, gate = mod[i, :, 0], mod[i, :, 1], mod[i, :, 2]
        j = i // 2
        if i % 2 == 0:
            res = _inproj(x, scale, shift, moba_w_in[j].astype(BF16), prev=pending, alpha=alpha)
            if pending is not None:
                x = res.pop(0)
            (proj,) = res
            o = _moba_attention(proj, heads=heads)
            pending = (o, proj, 3, gate, moba_w_out[j].astype(BF16), ln_g[i], ln_b[i])
        else:
            w_in = gdn_w_in[j]
            wab_t = w_in[:, 4 * width:].T.astype(BF16)
            res = _inproj(x, scale, shift, w_in[:, :4 * width].astype(BF16), wab_t, gdn_conv_w[j],
                          prev=pending, alpha=alpha)
            if pending is not None:
                x = res.pop(0)
            proj, abt = res
            o = _gdn(proj, abt, gdn_a_log[j], gdn_dt_bias[j], gdn_norm_w[j], heads=heads)
            pending = (o, None, 0, gate, gdn_w_out[j].astype(BF16), ln_g[i], ln_b[i])
    o, zsrc, z_off, gate, w_out, g, b = pending
    return _outproj_ln(o, x, gate, w_out, g, b, alpha=alpha, zsrc=zsrc, z_off=z_off)
```

```python
import functools

import jax
import jax.numpy as jnp
from jax import lax
from jax.experimental import pallas as pl
from jax.experimental.pallas import tpu as pltpu

F32 = jnp.float32
BF16 = jnp.bfloat16

HEAD_DIM = 128
MOBA_BLOCK = 256
MOBA_TOP_K = 3
GDN_CONV = 4
GDN_CHUNK = 64
LN_EPS = 1e-5
RMS_EPS = 1e-6
L2_EPS = 1e-6
NEG = -1e30
LANE = 128
SUBLANE_F32 = 8
SUBLANE_BF16 = 16
LOG2E = 1.4426950408889634
VMEM_LIMIT = 56 * 1024 * 1024
MOBA_HEADS_PER_STEP = 2
GDN_HEADS_PER_STEP = 4
GDN_CHUNKS_PER_ITER = 8


def _split3(x):
    x1 = x.astype(BF16)
    r1 = x - x1.astype(F32)
    x2 = r1.astype(BF16)
    x3 = (r1 - x2.astype(F32)).astype(BF16)
    return x1, x2, x3


def _dot(a, b):
    return jnp.dot(a, b, preferred_element_type=F32)


def _dot_nt(a, b):
    return lax.dot_general(a, b, (((1,), (1,)), ((), ())), preferred_element_type=F32)


def _dot_tn(a, b):
    return lax.dot_general(a, b, (((0,), (0,)), ((), ())), preferred_element_type=F32)


def _silu(x):
    return x * jax.nn.sigmoid(x)


def _mod_kernel(c_ref, w_ref, b_ref, o_ref):
    c = c_ref[...]
    cs = _silu(c)
    a1, a2, a3 = _split3(cs)
    b1, b2, b3 = _split3(w_ref[0])
    acc = _dot(a1, b1)
    acc += _dot(a1, b2) + _dot(a2, b1)
    acc += _dot(a1, b3) + _dot(a2, b2) + _dot(a3, b1)
    o_ref[0] = acc + b_ref[0]


def _modulation(c, ada_w, ada_b):
    depth, d, d3 = ada_w.shape
    bsz = c.shape[0]
    nt = d3 // d
    return pl.pallas_call(
        _mod_kernel,
        grid=(depth, nt),
        in_specs=[
            pl.BlockSpec((bsz, d), lambda i, j: (0, 0)),
            pl.BlockSpec((1, d, d), lambda i, j: (i, 0, j)),
            pl.BlockSpec((1, 1, d), lambda i, j: (i, 0, j)),
        ],
        out_specs=pl.BlockSpec((1, bsz, d), lambda i, j: (i, 0, j)),
        out_shape=jax.ShapeDtypeStruct((depth, bsz, d3), F32),
        compiler_params=pltpu.CompilerParams(vmem_limit_bytes=VMEM_LIMIT),
        name="adaln_mod",
    )(c, ada_w, ada_b.reshape(depth, 1, d3))


def _gated_outproj_ln(o_ref, z_ref, x, gate, w_ref, g_ref, b_ref, *, heads, alpha):
    cols = []
    for h in range(heads):
        if z_ref is not None:
            cols.append((o_ref[0, h].astype(F32) * _silu(z_ref[0, h].astype(F32))).astype(BF16))
        else:
            cols.append(o_ref[0, h])
    y = _dot(jnp.concatenate(cols, axis=-1), w_ref[...])
    r = alpha * x + gate * y
    mu = jnp.mean(r, axis=-1, keepdims=True)
    rc = r - mu
    var = jnp.mean(rc * rc, axis=-1, keepdims=True)
    return rc * lax.rsqrt(var + LN_EPS) * g_ref[...] + b_ref[...]


def _inproj_kernel(*refs, n_groups, group, gdn, fused, prev_gate_z, heads, alpha):
    refs = list(refs)
    if fused:
        po_ref = refs.pop(0)
        pz_ref = refs.pop(0) if prev_gate_z else None
        x_ref, pgate_ref, pw_ref, pg_ref, pb_ref = refs[:5]
        refs = refs[5:]
    else:
        x_ref = refs.pop(0)
    scale_ref, shift_ref, w_ref = refs[:3]
    refs = refs[3:]
    if gdn:
        wab_ref, conv_ref = refs[:2]
        refs = refs[2:]
    if fused:
        x1_ref = refs.pop(0)
    o_ref = refs.pop(0)
    n_conv = 0
    if gdn:
        abt_ref, carry_s = refs
        kconv = conv_ref.shape[0]
        halo = SUBLANE_F32
        n_conv = conv_ref.shape[1] // group

        @pl.when(pl.program_id(1) == 0)
        def _():
            carry_s[:, :halo, :] = jnp.zeros((n_conv, halo, group), F32)

    x = x_ref[0]
    if fused:
        x = _gated_outproj_ln(po_ref, pz_ref, x, pgate_ref[0], pw_ref, pg_ref, pb_ref, heads=heads, alpha=alpha)
        x1_ref[0] = x
    tm = x.shape[0]
    h = (x * (1.0 + scale_ref[0]) + shift_ref[0]).astype(BF16)
    per = group // LANE
    for n in range(n_groups):
        r = _dot(h, w_ref[:, n * group:(n + 1) * group])
        if n < n_conv:
            carry_s[n, halo:, :] = r
            cols = slice(n * group, (n + 1) * group)
            acc = r * conv_ref[kconv - 1:kconv, cols]
            for sh in range(1, kconv):
                acc = acc + carry_s[n, halo - sh:halo - sh + tm, :] * conv_ref[kconv - 1 - sh:kconv - sh, cols]
            carry_s[n, :halo, :] = r[tm - halo:]
            r = acc
        for j in range(per):
            o_ref[0, n * per + j] = r[:, j * LANE:(j + 1) * LANE].astype(BF16)
    if gdn:
        abt_ref[0] = _dot_nt(wab_ref[...], h)


def _inproj(x, scale, shift, w_bf16, wab_t=None, conv_w=None, prev=None, *, alpha=None, tm=512, group=512):
    bsz, s, d = x.shape
    n = w_bf16.shape[1]
    nblk = n // LANE
    gdn = wab_t is not None
    fused = prev is not None
    const = lambda b, t: (0, 0)
    row3 = pl.BlockSpec((1, 1, d), lambda b, t: (b, 0, 0))
    in_specs, args = [], []
    heads = None
    if fused:
        o, zsrc, z_off, gate, w_out, ln_g, ln_b = prev
        heads, dh = o.shape[1], o.shape[3]
        hblk = (1, heads, tm, dh)
        in_specs.append(pl.BlockSpec(hblk, lambda b, t: (b, 0, t, 0)))
        args.append(o)
        if zsrc is not None:
            in_specs.append(pl.BlockSpec(hblk, lambda b, t: (b, z_off, t, 0)))
            args.append(zsrc)
    in_specs.append(pl.BlockSpec((1, tm, d), lambda b, t: (b, t, 0)))
    args.append(x)
    if fused:
        in_specs += [row3, pl.BlockSpec((heads * dh, d), const), pl.BlockSpec((1, d), const),
                     pl.BlockSpec((1, d), const)]
        args += [gate, w_out, ln_g.reshape(1, d), ln_b.reshape(1, d)]
    in_specs += [row3, row3, pl.BlockSpec((d, n), const)]
    args += [scale, shift, w_bf16]
    out_specs, out_shape, scratch = [], [], []
    if gdn:
        nab = wab_t.shape[0]
        in_specs += [pl.BlockSpec((nab, d), const), pl.BlockSpec(conv_w.shape, const)]
        args += [wab_t, conv_w]
    if fused:
        out_specs.append(pl.BlockSpec((1, tm, d), lambda b, t: (b, t, 0)))
        out_shape.append(jax.ShapeDtypeStruct((bsz, s, d), F32))
    out_specs.append(pl.BlockSpec((1, nblk, tm, LANE), lambda b, t: (b, 0, t, 0)))
    out_shape.append(jax.ShapeDtypeStruct((bsz, nblk, s, LANE), BF16))
    if gdn:
        out_specs.append(pl.BlockSpec((1, nab, tm), lambda b, t: (b, 0, t)))
        out_shape.append(jax.ShapeDtypeStruct((bsz, nab, s), F32))
        scratch.append(pltpu.VMEM((conv_w.shape[1] // group, SUBLANE_F32 + tm, group), F32))
    return list(pl.pallas_call(
        functools.partial(_inproj_kernel, n_groups=n // group, group=group, gdn=gdn, fused=fused,
                          prev_gate_z=fused and prev[1] is not None, heads=heads, alpha=alpha),
        grid=(bsz, s // tm),
        in_specs=in_specs,
        out_specs=out_specs,
        out_shape=out_shape,
        scratch_shapes=scratch,
        compiler_params=pltpu.CompilerParams(
            dimension_semantics=("parallel", "arbitrary"), vmem_limit_bytes=VMEM_LIMIT),
        name=("outproj_" if fused else "") + ("inproj_conv" if gdn else "inproj"),
    )(*args))


def _moba_kernel(q_ref, k_ref, v_ref, o_ref, *, hp, nb, blk, top_k, scale):
    s, dh = q_ref.shape[2], q_ref.shape[3]
    row = lax.broadcasted_iota(jnp.int32, (blk, blk), 0)
    col = lax.broadcasted_iota(jnp.int32, (blk, blk), 1)
    causal = row <= col
    c2 = scale * LOG2E

    qs, ks, vts, gates = [], [], [], []
    for h in range(hp):
        q = q_ref[0, h]
        k = k_ref[0, h]
        k_mean = jnp.mean(k.astype(F32).reshape(nb, blk, dh), axis=1)
        m1, m2, m3 = _split3(k_mean)
        gates.append(_dot_nt(m1, q) + _dot_nt(m2, q) + _dot_nt(m3, q))
        qs.append((q.astype(F32) * c2).astype(BF16))
        ks.append(k)
        vts.append(jnp.concatenate([v_ref[0, h].astype(F32).T, jnp.ones((SUBLANE_BF16, s), F32)],
                                   axis=0).astype(BF16))

    def scores(h, i):
        return _dot_nt(ks[h][:(i + 1) * blk], qs[h][i * blk:(i + 1) * blk])

    def probs(h, i, s_t):
        parts = []
        if i > top_k:
            g = [gates[h][j:j + 1, i * blk:(i + 1) * blk] for j in range(i)]
        for j in range(i):
            sj = s_t[j * blk:(j + 1) * blk]
            if i > top_k:
                rank = jnp.zeros((1, blk), F32)
                for jj in range(i):
                    if jj == j:
                        continue
                    ahead = (g[jj] >= g[j]) if jj < j else (g[jj] > g[j])
                    rank = rank + jnp.where(ahead, 1.0, 0.0)
                sj = jnp.where(rank < float(top_k), sj, NEG)
            parts.append(sj)
        parts.append(jnp.where(causal, s_t[i * blk:], NEG))
        mx = parts[0].max(axis=0, keepdims=True)
        for p_ in parts[1:]:
            mx = jnp.maximum(mx, p_.max(axis=0, keepdims=True))
        return [jnp.exp2(p_ - mx).astype(BF16) for p_ in parts]

    def attend(h, i, es):
        acc = _dot(vts[h][:, :blk], es[0])
        for j in range(1, len(es)):
            acc = acc + _dot(vts[h][:, j * blk:(j + 1) * blk], es[j])
        o_t = acc[:dh] / acc[dh:dh + 1]
        o_ref[0, h, i * blk:(i + 1) * blk, :] = o_t.T.astype(BF16)

    s_next = [scores(h, 0) for h in range(hp)]
    for i in range(nb):
        s_cur = s_next
        if i + 1 < nb:
            s_next = [scores(h, i + 1) for h in range(hp)]
        es = [probs(h, i, s_cur[h]) for h in range(hp)]
        for h in range(hp):
            attend(h, i, es[h])


def _moba_attention(proj, *, heads):
    bsz, _, s, dh = proj.shape
    nb = s // MOBA_BLOCK
    hp = MOBA_HEADS_PER_STEP
    ngrp = heads // hp
    kern = functools.partial(_moba_kernel, hp=hp, nb=nb, blk=MOBA_BLOCK, top_k=MOBA_TOP_K,
                             scale=float(dh) ** -0.5)
    blk = (1, hp, s, dh)
    return pl.pallas_call(
        kern,
        grid=(bsz, ngrp),
        in_specs=[
            pl.BlockSpec(blk, lambda b, g: (b, g, 0, 0)),
            pl.BlockSpec(blk, lambda b, g: (b, ngrp + g, 0, 0)),
            pl.BlockSpec(blk, lambda b, g: (b, 2 * ngrp + g, 0, 0)),
        ],
        out_specs=pl.BlockSpec(blk, lambda b, g: (b, g, 0, 0)),
        out_shape=jax.ShapeDtypeStruct((bsz, heads, s, dh), BF16),
        compiler_params=pltpu.CompilerParams(
            dimension_semantics=("parallel", "parallel"), vmem_limit_bytes=VMEM_LIMIT),
        name="moba_attn",
    )(proj, proj, proj)


def _outproj_kernel(*refs, heads, alpha, gate_z):
    if gate_z:
        o_ref, z_ref, x_ref, gate_ref, w_ref, g_ref, b_ref, y_ref = refs
    else:
        o_ref, x_ref, gate_ref, w_ref, g_ref, b_ref, y_ref = refs
        z_ref = None
    y_ref[0] = _gated_outproj_ln(o_ref, z_ref, x_ref[0], gate_ref[0], w_ref, g_ref, b_ref,
                                 heads=heads, alpha=alpha)


def _outproj_ln(o, x, gate, w_bf16, ln_g, ln_b, *, alpha, zsrc=None, z_off=0, tm=1024):
    bsz, heads, s, dh = o.shape
    d = x.shape[-1]
    gate_z = zsrc is not None
    kern = functools.partial(_outproj_kernel, heads=heads, alpha=alpha, gate_z=gate_z)
    hblk = (1, heads, tm, dh)
    in_specs = [pl.BlockSpec(hblk, lambda b, t: (b, 0, t, 0))]
    args = [o]
    if gate_z:
        in_specs.append(pl.BlockSpec(hblk, lambda b, t: (b, z_off, t, 0)))
        args.append(zsrc)
    in_specs += [
        pl.BlockSpec((1, tm, d), lambda b, t: (b, t, 0)),
        pl.BlockSpec((1, 1, d), lambda b, t: (b, 0, 0)),
        pl.BlockSpec((heads * dh, d), lambda b, t: (0, 0)),
        pl.BlockSpec((1, d), lambda b, t: (0, 0)),
        pl.BlockSpec((1, d), lambda b, t: (0, 0)),
    ]
    args += [x, gate, w_bf16, ln_g.reshape(1, d), ln_b.reshape(1, d)]
    return pl.pallas_call(
        kern,
        grid=(bsz, s // tm),
        in_specs=in_specs,
        out_specs=pl.BlockSpec((1, tm, d), lambda b, t: (b, t, 0)),
        out_shape=jax.ShapeDtypeStruct((bsz, s, d), F32),
        compiler_params=pltpu.CompilerParams(
            dimension_semantics=("parallel", "parallel"), vmem_limit_bytes=VMEM_LIMIT),
        name="outproj_ln",
    )(*args)


def _gdn_kernel(q_ref, k_ref, v_ref, z_ref, a_ref, b_ref, alog_ref, dtb_ref, nw_ref, o_ref,
                gc_s, beta_s, alast_s, mq_s, n_s, oloc_s, state_s, *, hp, chunk, cg):
    s, dh = q_ref.shape[2], q_ref.shape[3]
    n_chunks = s // chunk
    grp = pl.program_id(1)

    r_i = lax.broadcasted_iota(jnp.int32, (chunk, chunk), 0)
    c_i = lax.broadcasted_iota(jnp.int32, (chunk, chunk), 1)
    eye = r_i == c_i
    incl = r_i >= c_i
    strict = r_i > c_i
    eye_f = jnp.where(eye, 1.0, 0.0)
    triu = jnp.where(r_i <= c_i, 1.0, 0.0).astype(BF16)

    for j in range(hp):
        head = grp * hp + j
        a_log = jnp.full((1, chunk), alog_ref[head], F32)
        dt_bias = jnp.full((1, chunk), dtb_ref[head], F32)
        g_rows = -jnp.exp(a_log) * jax.nn.softplus(a_ref[0, j] + dt_bias)
        g1, g2, g3 = _split3(g_rows)
        gc = _dot(g1, triu) + _dot(g2, triu) + _dot(g3, triu)
        gc_s[j] = gc
        beta_s[j] = jax.nn.sigmoid(b_ref[0, j])
        alast_s[j] = jnp.broadcast_to(jnp.exp(gc[:, chunk - 1:chunk]), (n_chunks, dh))

    def act(u_ref, j, r0):
        return _silu(u_ref[0, j, pl.ds(r0, chunk), :].astype(F32))

    def l2n(xv):
        return xv * lax.rsqrt(jnp.sum(xv * xv, axis=-1, keepdims=True) + L2_EPS)

    def to_col(row):
        return jnp.sum(jnp.where(eye, row, 0.0), axis=1, keepdims=True)

    def prep_phases(grp_idx):
        chains = [(j, grp_idx * cg + cc) for cc in range(cg) for j in range(hp)]
        rows0 = [pl.multiple_of(c * chunk, chunk) for _, c in chains]
        qs = [(l2n(act(q_ref, j, r0)) * (float(dh) ** -0.5)).astype(BF16) for (j, _), r0 in zip(chains, rows0)]
        ks = [l2n(act(k_ref, j, r0)).astype(BF16) for (j, _), r0 in zip(chains, rows0)]
        vs = [act(v_ref, j, r0) for (j, _), r0 in zip(chains, rows0)]
        gc_rows = [gc_s[j, pl.ds(c, 1), :] for j, c in chains]
        gc_cols = [to_col(r) for r in gc_rows]
        beta_cols = [to_col(beta_s[j, pl.ds(c, 1), :]) for j, c in chains]
        decs = [jnp.exp(jnp.where(incl, gcc - gcr, NEG)) for gcc, gcr in zip(gc_cols, gc_rows)]
        yield
        kqks = [_dot_nt(jnp.concatenate([k, q], axis=0), k) for q, k in zip(qs, ks)]
        yield
        xms = [-(kqk[:chunk] * (jnp.where(strict, d, 0.0) * bc)) for kqk, d, bc in zip(kqks, decs, beta_cols)]
        tinvs = [eye_f + x for x in xms]
        yield
        npow = 2
        while npow < chunk:
            xbs = [x.astype(BF16) for x in xms]
            xms = [_dot(xb, xb) for xb in xbs]
            tinvs = [t + _dot(t.astype(BF16), x.astype(BF16)) for t, x in zip(tinvs, xms)]
            npow *= 2
            yield
        rhss = [jnp.concatenate([v * bc, k.astype(F32) * (-bc * jnp.exp(gcc))], axis=1).astype(BF16)
                for v, k, bc, gcc in zip(vs, ks, beta_cols, gc_cols)]
        uws = [_dot(t.astype(BF16), r).astype(BF16) for t, r in zip(tinvs, rhss)]
        aqks = [(kqk[chunk:] * d).astype(BF16) for kqk, d in zip(kqks, decs)]
        a_uws = [_dot(a, uw) for a, uw in zip(aqks, uws)]
        kds = [(k.astype(F32) * jnp.exp(gr[:, chunk - 1:chunk] - gcc)).astype(BF16)
               for k, gr, gcc in zip(ks, gc_rows, gc_cols)]
        k_uws = [_dot_tn(kd, uw) for kd, uw in zip(kds, uws)]
        for idx, (j, c) in enumerate(chains):
            qp = qs[idx].astype(F32) * jnp.exp(gc_cols[idx]) + a_uws[idx][:, dh:]
            mq_s[j, c, :dh, :] = k_uws[idx][:, dh:].astype(BF16)
            mq_s[j, c, dh:, :] = qp.astype(BF16)
            n_s[j, c] = k_uws[idx][:, :dh]
            oloc_s[j, pl.ds(rows0[idx], chunk), :] = a_uws[idx][:, :dh]

    def finalize(j, c):
        r0 = pl.multiple_of(c * chunk, chunk)
        o = oloc_s[j, pl.ds(r0, chunk), :]
        o = o * lax.rsqrt(jnp.mean(o * o, axis=-1, keepdims=True) + RMS_EPS)
        o = o * nw_ref[...] * _silu(z_ref[0, j, pl.ds(r0, chunk), :].astype(F32))
        o_ref[0, j, pl.ds(r0, chunk), :] = o.astype(BF16)

    def scan_steps(grp_idx, states):
        for cc in range(cg):
            c = grp_idx * cg + cc
            r0 = pl.multiple_of(c * chunk, chunk)
            for j in range(hp):
                finalize(j, jnp.maximum(c - 1, 0))
            for j in range(hp):
                res = _dot(mq_s[j, c], states[j].astype(BF16))
                states[j] = states[j] * alast_s[j, pl.ds(c, 1), :] + res[:dh] + n_s[j, c]
                oloc_s[j, pl.ds(r0, chunk), :] += res[dh:]
            yield

    n_groups = n_chunks // cg
    for _ in prep_phases(0):
        pass

    def fused_body(g, carry):
        states = [state_s[j] for j in range(hp)]
        prep, scan = prep_phases(g + 1), scan_steps(g, states)
        live = True
        while live:
            live = next(prep, 0) is None
            next(scan, 0)
        for _ in scan:
            pass
        for j in range(hp):
            state_s[j] = states[j]
        return carry

    state_s[...] = jnp.zeros(state_s.shape, F32)
    lax.fori_loop(0, n_groups - 1, fused_body, 0)
    states = [state_s[j] for j in range(hp)]
    for _ in scan_steps(n_groups - 1, states):
        pass
    for j in range(hp):
        finalize(j, n_chunks - 1)


def _gdn(proj, abt, a_log, dt_bias, norm_w, *, heads):
    bsz, _, s, dh = proj.shape
    hp, cg, chunk = GDN_HEADS_PER_STEP, GDN_CHUNKS_PER_ITER, GDN_CHUNK
    n_chunks = s // chunk
    ngrp = heads // hp
    abt4 = abt.reshape(bsz, 2 * heads, n_chunks, chunk)
    kern = functools.partial(_gdn_kernel, hp=hp, chunk=chunk, cg=cg)
    blk = (1, hp, s, dh)
    rowblk = (1, hp, n_chunks, chunk)
    smem = pl.BlockSpec(memory_space=pltpu.SMEM)
    return pl.pallas_call(
        kern,
        grid=(bsz, ngrp),
        in_specs=[
            pl.BlockSpec(blk, lambda b, g: (b, g, 0, 0)),
            pl.BlockSpec(blk, lambda b, g: (b, ngrp + g, 0, 0)),
            pl.BlockSpec(blk, lambda b, g: (b, 2 * ngrp + g, 0, 0)),
            pl.BlockSpec(blk, lambda b, g: (b, 3 * ngrp + g, 0, 0)),
            pl.BlockSpec(rowblk, lambda b, g: (b, g, 0, 0)),
            pl.BlockSpec(rowblk, lambda b, g: (b, ngrp + g, 0, 0)),
            smem,
            smem,
            pl.BlockSpec((1, dh), lambda b, g: (0, 0)),
        ],
        out_specs=pl.BlockSpec(blk, lambda b, g: (b, g, 0, 0)),
        out_shape=jax.ShapeDtypeStruct((bsz, heads, s, dh), BF16),
        scratch_shapes=[
            pltpu.VMEM((hp, n_chunks, chunk), F32),
            pltpu.VMEM((hp, n_chunks, chunk), F32),
            pltpu.VMEM((hp, n_chunks, dh), F32),
            pltpu.VMEM((hp, n_chunks, dh + chunk, dh), BF16),
            pltpu.VMEM((hp, n_chunks, dh, dh), F32),
            pltpu.VMEM((hp, s, dh), F32),
            pltpu.VMEM((hp, dh, dh), F32),
        ],
        compiler_params=pltpu.CompilerParams(
            dimension_semantics=("parallel", "parallel"), vmem_limit_bytes=VMEM_LIMIT),
        name="gdn",
    )(proj, proj, proj, proj, abt4, abt4, a_log, dt_bias, norm_w.reshape(1, dh))


def kernel(x, c, ada_w, ada_b, ln_g, ln_b, moba_w_in, moba_w_out, gdn_w_in, gdn_conv_w, gdn_a_log,
           gdn_dt_bias, gdn_norm_w, gdn_w_out):
    bsz, s, d = x.shape
    depth = ada_w.shape[0]
    heads = d // HEAD_DIM
    width = heads * HEAD_DIM
    alpha = (2.0 * depth) ** 0.25

    mod = _modulation(c, ada_w, ada_b).reshape(depth, bsz, 3, 1, d)

    pending = None
    for i in range(depth):
        shift, scale, gate = mod[i, :, 0], mod[i, :, 1], mod[i, :, 2]
        j = i // 2
        if i % 2 == 0:
            res = _inproj(x, scale, shift, moba_w_in[j].astype(BF16), prev=pending, alpha=alpha)
            if pending is not None:
                x = res.pop(0)
            (proj,) = res
            o = _moba_attention(proj, heads=heads)
            pending = (o, proj, 3, gate, moba_w_out[j].astype(BF16), ln_g[i], ln_b[i])
        else:
            w_in = gdn_w_in[j]
            wab_t = w_in[:, 4 * width:].T.astype(BF16)
            res = _inproj(x, scale, shift, w_in[:, :4 * width].astype(BF16), wab_t, gdn_conv_w[j],
                          prev=pending, alpha=alpha)
            if pending is not None:
                x = res.pop(0)
            proj, abt = res
            o = _gdn(proj, abt, gdn_a_log[j], gdn_dt_bias[j], gdn_norm_w[j], heads=heads)
            pending = (o, None, 0, gate, gdn_w_out[j].astype(BF16), ln_g[i], ln_b[i])
    o, zsrc, z_off, gate, w_out, g, b = pending
    return _outproj_ln(o, x, gate, w_out, g, b, alpha=alpha, zsrc=zsrc, z_off=z_off)
```

```python
import functools

import jax
import jax.numpy as jnp
from jax import lax
from jax.experimental import pallas as pl
from jax.experimental.pallas import tpu as pltpu

F32 = jnp.float32
BF16 = jnp.bfloat16

HEAD_DIM = 128
MOBA_BLOCK = 256
MOBA_TOP_K = 3
GDN_CONV = 4
GDN_CHUNK = 64
LN_EPS = 1e-5
RMS_EPS = 1e-6
L2_EPS = 1e-6
NEG = -1e30
LANE = 128
SUBLANE_F32 = 8
SUBLANE_BF16 = 16
LOG2E = 1.4426950408889634
VMEM_LIMIT = 56 * 1024 * 1024
ROW_TILE = 512
MOBA_HEADS_PER_STEP = 2
GDN_HEADS_PER_STEP = 4
GDN_CHUNKS_PER_ITER = 8


def _split3(x):
    x1 = x.astype(BF16)
    r1 = x - x1.astype(F32)
    x2 = r1.astype(BF16)
    x3 = (r1 - x2.astype(F32)).astype(BF16)
    return x1, x2, x3


def _dot(a, b):
    return jnp.dot(a, b, preferred_element_type=F32)


def _dot_nt(a, b):
    return lax.dot_general(a, b, (((1,), (1,)), ((), ())), preferred_element_type=F32)


def _dot_tn(a, b):
    return lax.dot_general(a, b, (((0,), (0,)), ((), ())), preferred_element_type=F32)


def _silu(x):
    return x * jax.nn.sigmoid(x)


def _mod_kernel(c_ref, w_ref, b_ref, o_ref):
    c = c_ref[...]
    cs = _silu(c)
    a1, a2, a3 = _split3(cs)
    b1, b2, b3 = _split3(w_ref[0])
    acc = _dot(a1, b1)
    acc += _dot(a1, b2) + _dot(a2, b1)
    acc += _dot(a1, b3) + _dot(a2, b2) + _dot(a3, b1)
    o_ref[0] = acc + b_ref[0]


def _modulation(c, ada_w, ada_b):
    depth, d, d3 = ada_w.shape
    bsz = c.shape[0]
    nt = d3 // d
    return pl.pallas_call(
        _mod_kernel,
        grid=(depth, nt),
        in_specs=[
            pl.BlockSpec((bsz, d), lambda i, j: (0, 0)),
            pl.BlockSpec((1, d, d), lambda i, j: (i, 0, j)),
            pl.BlockSpec((1, 1, d), lambda i, j: (i, 0, j)),
        ],
        out_specs=pl.BlockSpec((1, bsz, d), lambda i, j: (i, 0, j)),
        out_shape=jax.ShapeDtypeStruct((depth, bsz, d3), F32),
        compiler_params=pltpu.CompilerParams(vmem_limit_bytes=VMEM_LIMIT),
        name="adaln_mod",
    )(c, ada_w, ada_b.reshape(depth, 1, d3))


def _gated_outproj_ln(o_ref, z_ref, x, gate, w_ref, g_ref, b_ref, nw_ref=None, *, heads, alpha):
    cols = []
    for h in range(heads):
        if z_ref is not None:
            o = o_ref[0, h].astype(F32)
            if nw_ref is not None:
                o = o * lax.rsqrt(jnp.mean(o * o, axis=-1, keepdims=True) + RMS_EPS) * nw_ref[...]
            cols.append((o * _silu(z_ref[0, h].astype(F32))).astype(BF16))
        else:
            cols.append(o_ref[0, h])
    y = _dot(jnp.concatenate(cols, axis=-1), w_ref[...])
    r = alpha * x + gate * y
    mu = jnp.mean(r, axis=-1, keepdims=True)
    rc = r - mu
    var = jnp.mean(rc * rc, axis=-1, keepdims=True)
    return rc * lax.rsqrt(var + LN_EPS) * g_ref[...] + b_ref[...]


def _inproj_kernel(*refs, n_groups, group, gdn, fused, prev_gate_z, prev_rms, heads, alpha, lead_cols,
                   lead_scale):
    refs = list(refs)
    if fused:
        po_ref = refs.pop(0)
        pz_ref = refs.pop(0) if prev_gate_z else None
        pnw_ref = refs.pop(0) if prev_rms else None
        x_ref, pgate_ref, pw_ref, pg_ref, pb_ref = refs[:5]
        refs = refs[5:]
    else:
        x_ref = refs.pop(0)
    scale_ref, shift_ref, w_ref = refs[:3]
    refs = refs[3:]
    if gdn:
        wab_ref, conv_ref = refs[:2]
        refs = refs[2:]
    if fused:
        x1_ref = refs.pop(0)
    o_ref = refs.pop(0)
    n_conv = 0
    if gdn:
        abt_ref, carry_s = refs
        kconv = conv_ref.shape[0]
        halo = SUBLANE_F32
        n_conv = conv_ref.shape[1] // group

        @pl.when(pl.program_id(1) == 0)
        def _():
            carry_s[:, :halo, :] = jnp.zeros((n_conv, halo, group), F32)

    x = x_ref[0]
    if fused:
        x = _gated_outproj_ln(po_ref, pz_ref, x, pgate_ref[0], pw_ref, pg_ref, pb_ref, pnw_ref,
                              heads=heads, alpha=alpha)
        x1_ref[0] = x
    tm = x.shape[0]
    h = (x * (1.0 + scale_ref[0]) + shift_ref[0]).astype(BF16)
    per = group // LANE
    for n in range(n_groups):
        r = _dot(h, w_ref[:, n * group:(n + 1) * group])
        if (n + 1) * group <= lead_cols:
            r = r * lead_scale
        if n < n_conv:
            carry_s[n, halo:, :] = r
            cols = slice(n * group, (n + 1) * group)
            acc = r * conv_ref[kconv - 1:kconv, cols]
            for sh in range(1, kconv):
                acc = acc + carry_s[n, halo - sh:halo - sh + tm, :] * conv_ref[kconv - 1 - sh:kconv - sh, cols]
            carry_s[n, :halo, :] = r[tm - halo:]
            r = acc
        for j in range(per):
            o_ref[0, n * per + j] = r[:, j * LANE:(j + 1) * LANE].astype(BF16)
    if gdn:
        abt_ref[0] = _dot_nt(wab_ref[...], h)


def _inproj(x, scale, shift, w_bf16, wab_t=None, conv_w=None, prev=None, *, alpha=None, group=512,
            lead_cols=0, lead_scale=1.0):
    assert lead_cols % group == 0
    bsz, s, d = x.shape
    n = w_bf16.shape[1]
    nblk = n // LANE
    gdn = wab_t is not None
    fused = prev is not None
    tm = ROW_TILE if (gdn or fused) else 2 * ROW_TILE
    const = lambda b, t: (0, 0)
    row3 = pl.BlockSpec((1, 1, d), lambda b, t: (b, 0, 0))
    in_specs, args = [], []
    heads = None
    prev_gate_z = prev_rms = False
    if fused:
        o, zsrc, z_off, gate, w_out, ln_g, ln_b, norm_w = prev
        prev_gate_z, prev_rms = zsrc is not None, norm_w is not None
        assert prev_gate_z or not prev_rms
        heads, dh = o.shape[1], o.shape[3]
        hblk = (1, heads, tm, dh)
        in_specs.append(pl.BlockSpec(hblk, lambda b, t: (b, 0, t, 0)))
        args.append(o)
        if prev_gate_z:
            in_specs.append(pl.BlockSpec(hblk, lambda b, t: (b, z_off, t, 0)))
            args.append(zsrc)
        if prev_rms:
            in_specs.append(pl.BlockSpec((1, dh), const))
            args.append(norm_w.reshape(1, dh))
    in_specs.append(pl.BlockSpec((1, tm, d), lambda b, t: (b, t, 0)))
    args.append(x)
    if fused:
        in_specs += [row3, pl.BlockSpec((heads * dh, d), const), pl.BlockSpec((1, d), const),
                     pl.BlockSpec((1, d), const)]
        args += [gate, w_out, ln_g.reshape(1, d), ln_b.reshape(1, d)]
    in_specs += [row3, row3, pl.BlockSpec((d, n), const)]
    args += [scale, shift, w_bf16]
    out_specs, out_shape, scratch = [], [], []
    if gdn:
        nab = wab_t.shape[0]
        in_specs += [pl.BlockSpec((nab, d), const), pl.BlockSpec(conv_w.shape, const)]
        args += [wab_t, conv_w]
    if fused:
        out_specs.append(pl.BlockSpec((1, tm, d), lambda b, t: (b, t, 0)))
        out_shape.append(jax.ShapeDtypeStruct((bsz, s, d), F32))
    out_specs.append(pl.BlockSpec((1, nblk, tm, LANE), lambda b, t: (b, 0, t, 0)))
    out_shape.append(jax.ShapeDtypeStruct((bsz, nblk, s, LANE), BF16))
    if gdn:
        out_specs.append(pl.BlockSpec((1, nab, tm), lambda b, t: (b, 0, t)))
        out_shape.append(jax.ShapeDtypeStruct((bsz, nab, s), F32))
        scratch.append(pltpu.VMEM((conv_w.shape[1] // group, SUBLANE_F32 + tm, group), F32))
    return list(pl.pallas_call(
        functools.partial(_inproj_kernel, n_groups=n // group, group=group, gdn=gdn, fused=fused,
                          prev_gate_z=prev_gate_z, prev_rms=prev_rms, heads=heads, alpha=alpha,
                          lead_cols=lead_cols, lead_scale=lead_scale),
        grid=(bsz, s // tm),
        in_specs=in_specs,
        out_specs=out_specs,
        out_shape=out_shape,
        scratch_shapes=scratch,
        compiler_params=pltpu.CompilerParams(
            dimension_semantics=("parallel", "arbitrary"), vmem_limit_bytes=VMEM_LIMIT),
        name=("outproj_" if fused else "") + ("inproj_conv" if gdn else "inproj"),
    )(*args))


def _moba_kernel(q_ref, k_ref, v_ref, o_ref, *, hp, nb, blk, top_k):
    s, dh = q_ref.shape[2], q_ref.shape[3]
    row = lax.broadcasted_iota(jnp.int32, (blk, blk), 0)
    col = lax.broadcasted_iota(jnp.int32, (blk, blk), 1)
    causal = row <= col

    qs, ks, vts, gates = [], [], [], []
    for h in range(hp):
        q = q_ref[0, h]
        k = k_ref[0, h]
        k_mean = jnp.mean(k.astype(F32).reshape(nb, blk, dh), axis=1)
        m1, m2, m3 = _split3(k_mean)
        gates.append(_dot_nt(m1, q) + _dot_nt(m2, q) + _dot_nt(m3, q))
        qs.append(q)
        ks.append(k)
        vts.append(jnp.concatenate([v_ref[0, h].astype(F32).T, jnp.ones((SUBLANE_BF16, s), F32)],
                                   axis=0).astype(BF16))

    def scores(h, i):
        return _dot_nt(ks[h][:(i + 1) * blk], qs[h][i * blk:(i + 1) * blk])

    def probs(h, i, s_t):
        parts = []
        if i > top_k:
            g = [gates[h][j:j + 1, i * blk:(i + 1) * blk] for j in range(i)]
        for j in range(i):
            sj = s_t[j * blk:(j + 1) * blk]
            if i > top_k:
                rank = jnp.zeros((1, blk), F32)
                for jj in range(i):
                    if jj == j:
                        continue
                    ahead = (g[jj] >= g[j]) if jj < j else (g[jj] > g[j])
                    rank = rank + jnp.where(ahead, 1.0, 0.0)
                sj = jnp.where(rank < float(top_k), sj, NEG)
            parts.append(sj)
        parts.append(jnp.where(causal, s_t[i * blk:], NEG))
        mx = parts[0].max(axis=0, keepdims=True)
        for p_ in parts[1:]:
            mx = jnp.maximum(mx, p_.max(axis=0, keepdims=True))
        return [jnp.exp2(p_ - mx).astype(BF16) for p_ in parts]

    def attend(h, i, es):
        acc = _dot(vts[h][:, :blk], es[0])
        for j in range(1, len(es)):
            acc = acc + _dot(vts[h][:, j * blk:(j + 1) * blk], es[j])
        o_t = acc[:dh] / acc[dh:dh + 1]
        o_ref[0, h, i * blk:(i + 1) * blk, :] = o_t.T.astype(BF16)

    s_next = [scores(h, 0) for h in range(hp)]
    for i in range(nb):
        s_cur = s_next
        if i + 1 < nb:
            s_next = [scores(h, i + 1) for h in range(hp)]
        es = [probs(h, i, s_cur[h]) for h in range(hp)]
        for h in range(hp):
            attend(h, i, es[h])


def _moba_attention(proj, *, heads):
    bsz, _, s, dh = proj.shape
    nb = s // MOBA_BLOCK
    hp = MOBA_HEADS_PER_STEP
    ngrp = heads // hp
    kern = functools.partial(_moba_kernel, hp=hp, nb=nb, blk=MOBA_BLOCK, top_k=MOBA_TOP_K)
    blk = (1, hp, s, dh)
    return pl.pallas_call(
        kern,
        grid=(bsz, ngrp),
        in_specs=[
            pl.BlockSpec(blk, lambda b, g: (b, g, 0, 0)),
            pl.BlockSpec(blk, lambda b, g: (b, ngrp + g, 0, 0)),
            pl.BlockSpec(blk, lambda b, g: (b, 2 * ngrp + g, 0, 0)),
        ],
        out_specs=pl.BlockSpec(blk, lambda b, g: (b, g, 0, 0)),
        out_shape=jax.ShapeDtypeStruct((bsz, heads, s, dh), BF16),
        compiler_params=pltpu.CompilerParams(
            dimension_semantics=("parallel", "parallel"), vmem_limit_bytes=VMEM_LIMIT),
        name="moba_attn",
    )(proj, proj, proj)


def _outproj_kernel(*refs, heads, alpha, gate_z, rms):
    refs = list(refs)
    o_ref = refs.pop(0)
    z_ref = refs.pop(0) if gate_z else None
    nw_ref = refs.pop(0) if rms else None
    x_ref, gate_ref, w_ref, g_ref, b_ref, y_ref = refs
    y_ref[0] = _gated_outproj_ln(o_ref, z_ref, x_ref[0], gate_ref[0], w_ref, g_ref, b_ref, nw_ref,
                                 heads=heads, alpha=alpha)


def _outproj_ln(o, x, gate, w_bf16, ln_g, ln_b, *, alpha, zsrc=None, z_off=0, norm_w=None, tm=2 * ROW_TILE):
    bsz, heads, s, dh = o.shape
    d = x.shape[-1]
    gate_z = zsrc is not None
    rms = norm_w is not None
    assert gate_z or not rms
    kern = functools.partial(_outproj_kernel, heads=heads, alpha=alpha, gate_z=gate_z, rms=rms)
    hblk = (1, heads, tm, dh)
    in_specs = [pl.BlockSpec(hblk, lambda b, t: (b, 0, t, 0))]
    args = [o]
    if gate_z:
        in_specs.append(pl.BlockSpec(hblk, lambda b, t: (b, z_off, t, 0)))
        args.append(zsrc)
    if rms:
        in_specs.append(pl.BlockSpec((1, dh), lambda b, t: (0, 0)))
        args.append(norm_w.reshape(1, dh))
    in_specs += [
        pl.BlockSpec((1, tm, d), lambda b, t: (b, t, 0)),
        pl.BlockSpec((1, 1, d), lambda b, t: (b, 0, 0)),
        pl.BlockSpec((heads * dh, d), lambda b, t: (0, 0)),
        pl.BlockSpec((1, d), lambda b, t: (0, 0)),
        pl.BlockSpec((1, d), lambda b, t: (0, 0)),
    ]
    args += [x, gate, w_bf16, ln_g.reshape(1, d), ln_b.reshape(1, d)]
    return pl.pallas_call(
        kern,
        grid=(bsz, s // tm),
        in_specs=in_specs,
        out_specs=pl.BlockSpec((1, tm, d), lambda b, t: (b, t, 0)),
        out_shape=jax.ShapeDtypeStruct((bsz, s, d), F32),
        compiler_params=pltpu.CompilerParams(
            dimension_semantics=("parallel", "parallel"), vmem_limit_bytes=VMEM_LIMIT),
        name="outproj_ln",
    )(*args)


def _gdn_kernel(q_ref, k_ref, v_ref, a_ref, b_ref, alog_ref, dtb_ref, o_ref,
                gc_s, beta_s, alast_s, mq_s, n_s, oloc_s, state_s, *, hp, chunk, cg):
    s, dh = q_ref.shape[2], q_ref.shape[3]
    n_chunks = s // chunk
    grp = pl.program_id(1)

    r_i = lax.broadcasted_iota(jnp.int32, (chunk, chunk), 0)
    c_i = lax.broadcasted_iota(jnp.int32, (chunk, chunk), 1)
    eye = r_i == c_i
    incl = r_i >= c_i
    strict = r_i > c_i
    eye_f = jnp.where(eye, 1.0, 0.0)
    triu = jnp.where(r_i <= c_i, 1.0, 0.0).astype(BF16)

    for j in range(hp):
        head = grp * hp + j
        a_log = jnp.full((1, chunk), alog_ref[head], F32)
        dt_bias = jnp.full((1, chunk), dtb_ref[head], F32)
        g_rows = -jnp.exp(a_log) * jax.nn.softplus(a_ref[0, j] + dt_bias)
        g1, g2, g3 = _split3(g_rows)
        gc = _dot(g1, triu) + _dot(g2, triu) + _dot(g3, triu)
        gc_s[j] = gc
        beta_s[j] = jax.nn.sigmoid(b_ref[0, j])
        alast_s[j] = jnp.broadcast_to(jnp.exp(gc[:, chunk - 1:chunk]), (n_chunks, dh))

    def act(u_ref, j, r0):
        return _silu(u_ref[0, j, pl.ds(r0, chunk), :].astype(F32))

    def l2n(xv):
        return xv * lax.rsqrt(jnp.sum(xv * xv, axis=-1, keepdims=True) + L2_EPS)

    def to_col(row):
        return jnp.sum(jnp.where(eye, row, 0.0), axis=1, keepdims=True)

    def prep_phases(grp_idx):
        chains = [(j, grp_idx * cg + cc) for cc in range(cg) for j in range(hp)]
        rows0 = [pl.multiple_of(c * chunk, chunk) for _, c in chains]
        qs = [(l2n(act(q_ref, j, r0)) * (float(dh) ** -0.5)).astype(BF16) for (j, _), r0 in zip(chains, rows0)]
        ks = [l2n(act(k_ref, j, r0)).astype(BF16) for (j, _), r0 in zip(chains, rows0)]
        vs = [act(v_ref, j, r0) for (j, _), r0 in zip(chains, rows0)]
        gc_rows = [gc_s[j, pl.ds(c, 1), :] for j, c in chains]
        gc_cols = [to_col(r) for r in gc_rows]
        beta_cols = [to_col(beta_s[j, pl.ds(c, 1), :]) for j, c in chains]
        decs = [jnp.exp(jnp.where(incl, gcc - gcr, NEG)) for gcc, gcr in zip(gc_cols, gc_rows)]
        yield
        kqks = [_dot_nt(jnp.concatenate([k, q], axis=0), k) for q, k in zip(qs, ks)]
        yield
        xms = [-(kqk[:chunk] * (jnp.where(strict, d, 0.0) * bc)) for kqk, d, bc in zip(kqks, decs, beta_cols)]
        tinvs = [eye_f + x for x in xms]
        yield
        npow = 2
        while npow < chunk:
            xbs = [x.astype(BF16) for x in xms]
            xms = [_dot(xb, xb) for xb in xbs]
            tinvs = [t + _dot(t.astype(BF16), x.astype(BF16)) for t, x in zip(tinvs, xms)]
            npow *= 2
            yield
        rhss = [jnp.concatenate([v * bc, k.astype(F32) * (-bc * jnp.exp(gcc))], axis=1).astype(BF16)
                for v, k, bc, gcc in zip(vs, ks, beta_cols, gc_cols)]
        uws = [_dot(t.astype(BF16), r).astype(BF16) for t, r in zip(tinvs, rhss)]
        aqks = [(kqk[chunk:] * d).astype(BF16) for kqk, d in zip(kqks, decs)]
        a_uws = [_dot(a, uw) for a, uw in zip(aqks, uws)]
        kds = [(k.astype(F32) * jnp.exp(gr[:, chunk - 1:chunk] - gcc)).astype(BF16)
               for k, gr, gcc in zip(ks, gc_rows, gc_cols)]
        k_uws = [_dot_tn(kd, uw) for kd, uw in zip(kds, uws)]
        for idx, (j, c) in enumerate(chains):
            qp = qs[idx].astype(F32) * jnp.exp(gc_cols[idx]) + a_uws[idx][:, dh:]
            mq_s[j, c, :dh, :] = k_uws[idx][:, dh:].astype(BF16)
            mq_s[j, c, dh:, :] = qp.astype(BF16)
            n_s[j, c] = k_uws[idx][:, :dh]
            oloc_s[j, pl.ds(rows0[idx], chunk), :] = a_uws[idx][:, :dh]

    def finalize(j, c):
        r0 = pl.multiple_of(c * chunk, chunk)
        o_ref[0, j, pl.ds(r0, chunk), :] = oloc_s[j, pl.ds(r0, chunk), :].astype(BF16)

    def scan_steps(grp_idx, states):
        for cc in range(cg):
            c = grp_idx * cg + cc
            r0 = pl.multiple_of(c * chunk, chunk)
            for j in range(hp):
                finalize(j, jnp.maximum(c - 1, 0))
            for j in range(hp):
                res = _dot(mq_s[j, c], states[j].astype(BF16))
                states[j] = states[j] * alast_s[j, pl.ds(c, 1), :] + res[:dh] + n_s[j, c]
                oloc_s[j, pl.ds(r0, chunk), :] += res[dh:]
            yield

    n_groups = n_chunks // cg
    for _ in prep_phases(0):
        pass

    def fused_body(g, carry):
        states = [state_s[j] for j in range(hp)]
        prep, scan = prep_phases(g + 1), scan_steps(g, states)
        live = True
        while live:
            live = next(prep, 0) is None
            next(scan, 0)
        for _ in scan:
            pass
        for j in range(hp):
            state_s[j] = states[j]
        return carry

    state_s[...] = jnp.zeros(state_s.shape, F32)
    lax.fori_loop(0, n_groups - 1, fused_body, 0)
    states = [state_s[j] for j in range(hp)]
    for _ in scan_steps(n_groups - 1, states):
        pass
    for j in range(hp):
        finalize(j, n_chunks - 1)


def _gdn(proj, abt, a_log, dt_bias, *, heads):
    bsz, _, s, dh = proj.shape
    hp, cg, chunk = GDN_HEADS_PER_STEP, GDN_CHUNKS_PER_ITER, GDN_CHUNK
    n_chunks = s // chunk
    ngrp = heads // hp
    abt4 = abt.reshape(bsz, 2 * heads, n_chunks, chunk)
    kern = functools.partial(_gdn_kernel, hp=hp, chunk=chunk, cg=cg)
    blk = (1, hp, s, dh)
    rowblk = (1, hp, n_chunks, chunk)
    smem = pl.BlockSpec(memory_space=pltpu.SMEM)
    return pl.pallas_call(
        kern,
        grid=(bsz, ngrp),
        in_specs=[
            pl.BlockSpec(blk, lambda b, g: (b, g, 0, 0)),
            pl.BlockSpec(blk, lambda b, g: (b, ngrp + g, 0, 0)),
            pl.BlockSpec(blk, lambda b, g: (b, 2 * ngrp + g, 0, 0)),
            pl.BlockSpec(rowblk, lambda b, g: (b, g, 0, 0)),
            pl.BlockSpec(rowblk, lambda b, g: (b, ngrp + g, 0, 0)),
            smem,
            smem,
        ],
        out_specs=pl.BlockSpec(blk, lambda b, g: (b, g, 0, 0)),
        out_shape=jax.ShapeDtypeStruct((bsz, heads, s, dh), BF16),
        scratch_shapes=[
            pltpu.VMEM((hp, n_chunks, chunk), F32),
            pltpu.VMEM((hp, n_chunks, chunk), F32),
            pltpu.VMEM((hp, n_chunks, dh), F32),
            pltpu.VMEM((hp, n_chunks, dh + chunk, dh), BF16),
            pltpu.VMEM((hp, n_chunks, dh, dh), F32),
            pltpu.VMEM((hp, s, dh), F32),
            pltpu.VMEM((hp, dh, dh), F32),
        ],
        compiler_params=pltpu.CompilerParams(
            dimension_semantics=("parallel", "parallel"), vmem_limit_bytes=VMEM_LIMIT),
        name="gdn",
    )(proj, proj, proj, abt4, abt4, a_log, dt_bias)


def kernel(x, c, ada_w, ada_b, ln_g, ln_b, moba_w_in, moba_w_out, gdn_w_in, gdn_conv_w, gdn_a_log,
           gdn_dt_bias, gdn_norm_w, gdn_w_out):
    bsz, s, d = x.shape
    depth = ada_w.shape[0]
    heads = d // HEAD_DIM
    width = heads * HEAD_DIM
    alpha = (2.0 * depth) ** 0.25

    mod = _modulation(c, ada_w, ada_b).reshape(depth, bsz, 3, 1, d)

    pending = None
    for i in range(depth):
        shift, scale, gate = mod[i, :, 0], mod[i, :, 1], mod[i, :, 2]
        j = i // 2
        if i % 2 == 0:
            res = _inproj(x, scale, shift, moba_w_in[j].astype(BF16), prev=pending, alpha=alpha,
                          lead_cols=width, lead_scale=float(HEAD_DIM) ** -0.5 * LOG2E)
            if pending is not None:
                x = res.pop(0)
            (proj,) = res
            o = _moba_attention(proj, heads=heads)
            pending = (o, proj, 3, gate, moba_w_out[j].astype(BF16), ln_g[i], ln_b[i], None)
        else:
            w_in = gdn_w_in[j]
            wab_t = w_in[:, 4 * width:].T.astype(BF16)
            res = _inproj(x, scale, shift, w_in[:, :4 * width].astype(BF16), wab_t, gdn_conv_w[j],
                          prev=pending, alpha=alpha)
            if pending is not None:
                x = res.pop(0)
            proj, abt = res
            o = _gdn(proj, abt, gdn_a_log[j], gdn_dt_bias[j], heads=heads)
            pending = (o, proj, 3, gate, gdn_w_out[j].astype(BF16), ln_g[i], ln_b[i], gdn_norm_w[j])
    o, zsrc, z_off, gate, w_out, g, b, norm_w = pending
    return _outproj_ln(o, x, gate, w_out, g, b, alpha=alpha, zsrc=zsrc, z_off=z_off, norm_w=norm_w)
```

```python
import functools

import jax
import jax.numpy as jnp
from jax import lax
from jax.experimental import pallas as pl
from jax.experimental.pallas import tpu as pltpu

F32 = jnp.float32
BF16 = jnp.bfloat16

HEAD_DIM = 128
MOBA_BLOCK = 256
MOBA_TOP_K = 3
GDN_CONV = 4
GDN_CHUNK = 64
LN_EPS = 1e-5
RMS_EPS = 1e-6
L2_EPS = 1e-6
NEG = -1e30
LANE = 128
SUBLANE_F32 = 8
SUBLANE_BF16 = 16
LOG2E = 1.4426950408889634
VMEM_LIMIT = 56 * 1024 * 1024
ROW_TILE = 512
OUTPROJ_SLABS = 4
MOBA_HEADS_PER_STEP = 2
GDN_HEADS_PER_STEP = 4
GDN_CHUNKS_PER_ITER = 8


def _split3(x):
    x1 = x.astype(BF16)
    r1 = x - x1.astype(F32)
    x2 = r1.astype(BF16)
    x3 = (r1 - x2.astype(F32)).astype(BF16)
    return x1, x2, x3


def _dot(a, b):
    return jnp.dot(a, b, preferred_element_type=F32)


def _dot_nt(a, b):
    return lax.dot_general(a, b, (((1,), (1,)), ((), ())), preferred_element_type=F32)


def _dot_tn(a, b):
    return lax.dot_general(a, b, (((0,), (0,)), ((), ())), preferred_element_type=F32)


def _silu(x):
    return x * jax.nn.sigmoid(x)


def _mod_kernel(c_ref, w_ref, b_ref, o_ref):
    c = c_ref[...]
    cs = _silu(c)
    a1, a2, a3 = _split3(cs)
    b1, b2, b3 = _split3(w_ref[0])
    acc = _dot(a1, b1)
    acc += _dot(a1, b2) + _dot(a2, b1)
    acc += _dot(a1, b3) + _dot(a2, b2) + _dot(a3, b1)
    o_ref[0] = acc + b_ref[0]


def _modulation(c, ada_w, ada_b):
    depth, d, d3 = ada_w.shape
    bsz = c.shape[0]
    nt = d3 // d
    return pl.pallas_call(
        _mod_kernel,
        grid=(depth, nt),
        in_specs=[
            pl.BlockSpec((bsz, d), lambda i, j: (0, 0)),
            pl.BlockSpec((1, d, d), lambda i, j: (i, 0, j)),
            pl.BlockSpec((1, 1, d), lambda i, j: (i, 0, j)),
        ],
        out_specs=pl.BlockSpec((1, bsz, d), lambda i, j: (i, 0, j)),
        out_shape=jax.ShapeDtypeStruct((depth, bsz, d3), F32),
        compiler_params=pltpu.CompilerParams(vmem_limit_bytes=VMEM_LIMIT),
        name="adaln_mod",
    )(c, ada_w, ada_b.reshape(depth, 1, d3))


def _gated_outproj_ln(o_ref, z_ref, x, gate, w_ref, g_ref, b_ref, nw_ref=None, *, heads, alpha, splits=1):
    rows = x.shape[0] // splits
    ys = []
    for sp in range(splits):
        sl = slice(sp * rows, (sp + 1) * rows)
        cols = []
        for h in range(heads):
            if z_ref is not None:
                o = o_ref[0, h, sl, :].astype(F32)
                if nw_ref is not None:
                    o = o * lax.rsqrt(jnp.mean(o * o, axis=-1, keepdims=True) + RMS_EPS) * nw_ref[...]
                cols.append((o * _silu(z_ref[0, h, sl, :].astype(F32))).astype(BF16))
            else:
                cols.append(o_ref[0, h, sl, :])
        ys.append(_dot(jnp.concatenate(cols, axis=-1), w_ref[...]))
    outs = []
    for sp in range(splits):
        r = alpha * x[sp * rows:(sp + 1) * rows] + gate * ys[sp]
        mu = jnp.mean(r, axis=-1, keepdims=True)
        rc = r - mu
        var = jnp.mean(rc * rc, axis=-1, keepdims=True)
        outs.append(rc * lax.rsqrt(var + LN_EPS) * g_ref[...] + b_ref[...])
    return outs[0] if splits == 1 else jnp.concatenate(outs, axis=0)


def _inproj_kernel(*refs, n_groups, group, gdn, fused, prev_gate_z, prev_rms, heads, alpha, lead_cols,
                   lead_scale):
    refs = list(refs)
    if fused:
        po_ref = refs.pop(0)
        pz_ref = refs.pop(0) if prev_gate_z else None
        pnw_ref = refs.pop(0) if prev_rms else None
        x_ref, pgate_ref, pw_ref, pg_ref, pb_ref = refs[:5]
        refs = refs[5:]
    else:
        x_ref = refs.pop(0)
    scale_ref, shift_ref, w_ref = refs[:3]
    refs = refs[3:]
    if gdn:
        wab_ref, conv_ref = refs[:2]
        refs = refs[2:]
    if fused:
        x1_ref = refs.pop(0)
    o_ref = refs.pop(0)
    n_conv = 0
    if gdn:
        abt_ref, carry_s = refs
        kconv = conv_ref.shape[0]
        halo = SUBLANE_F32
        n_conv = conv_ref.shape[1] // group

        @pl.when(pl.program_id(1) == 0)
        def _():
            carry_s[:, :halo, :] = jnp.zeros((n_conv, halo, group), F32)

    x = x_ref[0]
    if fused:
        x = _gated_outproj_ln(po_ref, pz_ref, x, pgate_ref[0], pw_ref, pg_ref, pb_ref, pnw_ref,
                              heads=heads, alpha=alpha, splits=OUTPROJ_SLABS // 2)
        x1_ref[0] = x
    tm = x.shape[0]
    h = (x * (1.0 + scale_ref[0]) + shift_ref[0]).astype(BF16)
    per = group // LANE
    for n in range(n_groups):
        r = _dot(h, w_ref[:, n * group:(n + 1) * group])
        if (n + 1) * group <= lead_cols:
            r = r * lead_scale
        if n < n_conv:
            carry_s[n, halo:, :] = r
            cols = slice(n * group, (n + 1) * group)
            acc = r * conv_ref[kconv - 1:kconv, cols]
            for sh in range(1, kconv):
                acc = acc + carry_s[n, halo - sh:halo - sh + tm, :] * conv_ref[kconv - 1 - sh:kconv - sh, cols]
            carry_s[n, :halo, :] = r[tm - halo:]
            r = acc
        for j in range(per):
            o_ref[0, n * per + j] = r[:, j * LANE:(j + 1) * LANE].astype(BF16)
    if gdn:
        abt_ref[0] = _dot_nt(wab_ref[...], h)


def _inproj(x, scale, shift, w_bf16, wab_t=None, conv_w=None, prev=None, *, alpha=None, group=512,
            lead_cols=0, lead_scale=1.0):
    assert lead_cols % group == 0
    bsz, s, d = x.shape
    n = w_bf16.shape[1]
    nblk = n // LANE
    gdn = wab_t is not None
    fused = prev is not None
    tm = ROW_TILE if (gdn or fused) else 2 * ROW_TILE
    const = lambda b, t: (0, 0)
    row3 = pl.BlockSpec((1, 1, d), lambda b, t: (b, 0, 0))
    in_specs, args = [], []
    heads = None
    prev_gate_z = prev_rms = False
    if fused:
        o, zsrc, z_off, gate, w_out, ln_g, ln_b, norm_w = prev
        prev_gate_z, prev_rms = zsrc is not None, norm_w is not None
        assert prev_gate_z or not prev_rms
        heads, dh = o.shape[1], o.shape[3]
        hblk = (1, heads, tm, dh)
        in_specs.append(pl.BlockSpec(hblk, lambda b, t: (b, 0, t, 0)))
        args.append(o)
        if prev_gate_z:
            in_specs.append(pl.BlockSpec(hblk, lambda b, t: (b, z_off, t, 0)))
            args.append(zsrc)
        if prev_rms:
            in_specs.append(pl.BlockSpec((1, dh), const))
            args.append(norm_w.reshape(1, dh))
    in_specs.append(pl.BlockSpec((1, tm, d), lambda b, t: (b, t, 0)))
    args.append(x)
    if fused:
        in_specs += [row3, pl.BlockSpec((heads * dh, d), const), pl.BlockSpec((1, d), const),
                     pl.BlockSpec((1, d), const)]
        args += [gate, w_out, ln_g.reshape(1, d), ln_b.reshape(1, d)]
    in_specs += [row3, row3, pl.BlockSpec((d, n), const)]
    args += [scale, shift, w_bf16]
    out_specs, out_shape, scratch = [], [], []
    if gdn:
        nab = wab_t.shape[0]
        in_specs += [pl.BlockSpec((nab, d), const), pl.BlockSpec(conv_w.shape, const)]
        args += [wab_t, conv_w]
    if fused:
        out_specs.append(pl.BlockSpec((1, tm, d), lambda b, t: (b, t, 0)))
        out_shape.append(jax.ShapeDtypeStruct((bsz, s, d), F32))
    out_specs.append(pl.BlockSpec((1, nblk, tm, LANE), lambda b, t: (b, 0, t, 0)))
    out_shape.append(jax.ShapeDtypeStruct((bsz, nblk, s, LANE), BF16))
    if gdn:
        out_specs.append(pl.BlockSpec((1, nab, tm), lambda b, t: (b, 0, t)))
        out_shape.append(jax.ShapeDtypeStruct((bsz, nab, s), F32))
        scratch.append(pltpu.VMEM((conv_w.shape[1] // group, SUBLANE_F32 + tm, group), F32))
    return list(pl.pallas_call(
        functools.partial(_inproj_kernel, n_groups=n // group, group=group, gdn=gdn, fused=fused,
                          prev_gate_z=prev_gate_z, prev_rms=prev_rms, heads=heads, alpha=alpha,
                          lead_cols=lead_cols, lead_scale=lead_scale),
        grid=(bsz, s // tm),
        in_specs=in_specs,
        out_specs=out_specs,
        out_shape=out_shape,
        scratch_shapes=scratch,
        compiler_params=pltpu.CompilerParams(
            dimension_semantics=("parallel", "arbitrary"), vmem_limit_bytes=VMEM_LIMIT),
        name=("outproj_" if fused else "") + ("inproj_conv" if gdn else "inproj"),
    )(*args))


def _moba_kernel(q_ref, k_ref, v_ref, o_ref, *, hp, nb, blk, top_k):
    s, dh = q_ref.shape[2], q_ref.shape[3]
    row = lax.broadcasted_iota(jnp.int32, (blk, blk), 0)
    col = lax.broadcasted_iota(jnp.int32, (blk, blk), 1)
    causal = row <= col

    qs, ks, vts, gates = [], [], [], []
    for h in range(hp):
        q = q_ref[0, h]
        k = k_ref[0, h]
        k_mean = jnp.mean(k.astype(F32).reshape(nb, blk, dh), axis=1)
        m1, m2, m3 = _split3(k_mean)
        gates.append(_dot_nt(m1, q) + _dot_nt(m2, q) + _dot_nt(m3, q))
        qs.append(q)
        ks.append(k)
        vts.append(jnp.concatenate([v_ref[0, h].astype(F32).T, jnp.ones((SUBLANE_BF16, s), F32)],
                                   axis=0).astype(BF16))

    def scores(h, i):
        return _dot_nt(ks[h][:(i + 1) * blk], qs[h][i * blk:(i + 1) * blk])

    def probs(h, i, s_t):
        parts = []
        if i > top_k:
            g = [gates[h][j:j + 1, i * blk:(i + 1) * blk] for j in range(i)]
        for j in range(i):
            sj = s_t[j * blk:(j + 1) * blk]
            if i > top_k:
                rank = jnp.zeros((1, blk), F32)
                for jj in range(i):
                    if jj == j:
                        continue
                    ahead = (g[jj] >= g[j]) if jj < j else (g[jj] > g[j])
                    rank = rank + jnp.where(ahead, 1.0, 0.0)
                sj = jnp.where(rank < float(top_k), sj, NEG)
            parts.append(sj)
        parts.append(jnp.where(causal, s_t[i * blk:], NEG))
        mx = parts[0].max(axis=0, keepdims=True)
        for p_ in parts[1:]:
            mx = jnp.maximum(mx, p_.max(axis=0, keepdims=True))
        return [jnp.exp2(p_ - mx).astype(BF16) for p_ in parts]

    def attend(h, i, es):
        acc = _dot(vts[h][:, :blk], es[0])
        for j in range(1, len(es)):
            acc = acc + _dot(vts[h][:, j * blk:(j + 1) * blk], es[j])
        o_t = acc[:dh] / acc[dh:dh + 1]
        o_ref[0, h, i * blk:(i + 1) * blk, :] = o_t.T.astype(BF16)

    s_next = [scores(h, 0) for h in range(hp)]
    for i in range(nb):
        s_cur = s_next
        if i + 1 < nb:
            s_next = [scores(h, i + 1) for h in range(hp)]
        es = [probs(h, i, s_cur[h]) for h in range(hp)]
        for h in range(hp):
            attend(h, i, es[h])


def _moba_attention(proj, *, heads):
    bsz, _, s, dh = proj.shape
    nb = s // MOBA_BLOCK
    hp = MOBA_HEADS_PER_STEP
    ngrp = heads // hp
    kern = functools.partial(_moba_kernel, hp=hp, nb=nb, blk=MOBA_BLOCK, top_k=MOBA_TOP_K)
    blk = (1, hp, s, dh)
    return pl.pallas_call(
        kern,
        grid=(bsz, ngrp),
        in_specs=[
            pl.BlockSpec(blk, lambda b, g: (b, g, 0, 0)),
            pl.BlockSpec(blk, lambda b, g: (b, ngrp + g, 0, 0)),
            pl.BlockSpec(blk, lambda b, g: (b, 2 * ngrp + g, 0, 0)),
        ],
        out_specs=pl.BlockSpec(blk, lambda b, g: (b, g, 0, 0)),
        out_shape=jax.ShapeDtypeStruct((bsz, heads, s, dh), BF16),
        compiler_params=pltpu.CompilerParams(
            dimension_semantics=("parallel", "parallel"), vmem_limit_bytes=VMEM_LIMIT),
        name="moba_attn",
    )(proj, proj, proj)


def _outproj_kernel(*refs, heads, alpha, gate_z, rms):
    refs = list(refs)
    o_ref = refs.pop(0)
    z_ref = refs.pop(0) if gate_z else None
    nw_ref = refs.pop(0) if rms else None
    x_ref, gate_ref, w_ref, g_ref, b_ref, y_ref = refs
    y_ref[0] = _gated_outproj_ln(o_ref, z_ref, x_ref[0], gate_ref[0], w_ref, g_ref, b_ref, nw_ref,
                                 heads=heads, alpha=alpha, splits=OUTPROJ_SLABS)


def _outproj_ln(o, x, gate, w_bf16, ln_g, ln_b, *, alpha, zsrc=None, z_off=0, norm_w=None, tm=2 * ROW_TILE):
    bsz, heads, s, dh = o.shape
    d = x.shape[-1]
    gate_z = zsrc is not None
    rms = norm_w is not None
    assert gate_z or not rms
    kern = functools.partial(_outproj_kernel, heads=heads, alpha=alpha, gate_z=gate_z, rms=rms)
    hblk = (1, heads, tm, dh)
    in_specs = [pl.BlockSpec(hblk, lambda b, t: (b, 0, t, 0))]
    args = [o]
    if gate_z:
        in_specs.append(pl.BlockSpec(hblk, lambda b, t: (b, z_off, t, 0)))
        args.append(zsrc)
    if rms:
        in_specs.append(pl.BlockSpec((1, dh), lambda b, t: (0, 0)))
        args.append(norm_w.reshape(1, dh))
    in_specs += [
        pl.BlockSpec((1, tm, d), lambda b, t: (b, t, 0)),
        pl.BlockSpec((1, 1, d), lambda b, t: (b, 0, 0)),
        pl.BlockSpec((heads * dh, d), lambda b, t: (0, 0)),
        pl.BlockSpec((1, d), lambda b, t: (0, 0)),
        pl.BlockSpec((1, d), lambda b, t: (0, 0)),
    ]
    args += [x, gate, w_bf16, ln_g.reshape(1, d), ln_b.reshape(1, d)]
    return pl.pallas_call(
        kern,
        grid=(bsz, s // tm),
        in_specs=in_specs,
        out_specs=pl.BlockSpec((1, tm, d), lambda b, t: (b, t, 0)),
        out_shape=jax.ShapeDtypeStruct((bsz, s, d), F32),
        compiler_params=pltpu.CompilerParams(
            dimension_semantics=("parallel", "parallel"), vmem_limit_bytes=VMEM_LIMIT),
        name="outproj_ln",
    )(*args)


def _gdn_kernel(q_ref, k_ref, v_ref, a_ref, b_ref, alog_ref, dtb_ref, o_ref,
                gc_s, beta_s, alast_s, mq_s, n_s, oloc_s, state_s, *, hp, chunk, cg):
    s, dh = q_ref.shape[2], q_ref.shape[3]
    n_chunks = s // chunk
    grp = pl.program_id(1)

    r_i = lax.broadcasted_iota(jnp.int32, (chunk, chunk), 0)
    c_i = lax.broadcasted_iota(jnp.int32, (chunk, chunk), 1)
    eye = r_i == c_i
    incl = r_i >= c_i
    strict = r_i > c_i
    eye_f = jnp.where(eye, 1.0, 0.0)
    triu = jnp.where(r_i <= c_i, 1.0, 0.0).astype(BF16)

    for j in range(hp):
        head = grp * hp + j
        a_log = jnp.full((1, chunk), alog_ref[head], F32)
        dt_bias = jnp.full((1, chunk), dtb_ref[head], F32)
        g_rows = -jnp.exp(a_log) * jax.nn.softplus(a_ref[0, j] + dt_bias)
        g1, g2, g3 = _split3(g_rows)
        gc = _dot(g1, triu) + _dot(g2, triu) + _dot(g3, triu)
        gc_s[j] = gc
        beta_s[j] = jax.nn.sigmoid(b_ref[0, j])
        alast_s[j] = jnp.broadcast_to(jnp.exp(gc[:, chunk - 1:chunk]), (n_chunks, dh))

    def act(u_ref, j, r0):
        return _silu(u_ref[0, j, pl.ds(r0, chunk), :].astype(F32))

    def l2n(xv):
        return xv * lax.rsqrt(jnp.sum(xv * xv, axis=-1, keepdims=True) + L2_EPS)

    def to_col(row):
        return jnp.sum(jnp.where(eye, row, 0.0), axis=1, keepdims=True)

    def prep_phases(grp_idx):
        chains = [(j, grp_idx * cg + cc) for cc in range(cg) for j in range(hp)]
        rows0 = [pl.multiple_of(c * chunk, chunk) for _, c in chains]
        qs = [(l2n(act(q_ref, j, r0)) * (float(dh) ** -0.5)).astype(BF16) for (j, _), r0 in zip(chains, rows0)]
        ks = [l2n(act(k_ref, j, r0)).astype(BF16) for (j, _), r0 in zip(chains, rows0)]
        vs = [act(v_ref, j, r0) for (j, _), r0 in zip(chains, rows0)]
        gc_rows = [gc_s[j, pl.ds(c, 1), :] for j, c in chains]
        gc_cols = [to_col(r) for r in gc_rows]
        beta_cols = [to_col(beta_s[j, pl.ds(c, 1), :]) for j, c in chains]
        decs = [jnp.exp(jnp.where(incl, gcc - gcr, NEG)) for gcc, gcr in zip(gc_cols, gc_rows)]
        yield
        kqks = [_dot_nt(jnp.concatenate([k, q], axis=0), k) for q, k in zip(qs, ks)]
        yield
        xms = [-(kqk[:chunk] * (jnp.where(strict, d, 0.0) * bc)) for kqk, d, bc in zip(kqks, decs, beta_cols)]
        tinvs = [eye_f + x for x in xms]
        yield
        npow = 2
        while npow < chunk:
            xbs = [x.astype(BF16) for x in xms]
            xms = [_dot(xb, xb) for xb in xbs]
            tinvs = [t + _dot(t.astype(BF16), x.astype(BF16)) for t, x in zip(tinvs, xms)]
            npow *= 2
            yield
        rhss = [jnp.concatenate([v * bc, k.astype(F32) * (-bc * jnp.exp(gcc))], axis=1).astype(BF16)
                for v, k, bc, gcc in zip(vs, ks, beta_cols, gc_cols)]
        uws = [_dot(t.astype(BF16), r).astype(BF16) for t, r in zip(tinvs, rhss)]
        aqks = [(kqk[chunk:] * d).astype(BF16) for kqk, d in zip(kqks, decs)]
        a_uws = [_dot(a, uw) for a, uw in zip(aqks, uws)]
        kds = [(k.astype(F32) * jnp.exp(gr[:, chunk - 1:chunk] - gcc)).astype(BF16)
               for k, gr, gcc in zip(ks, gc_rows, gc_cols)]
        k_uws = [_dot_tn(kd, uw) for kd, uw in zip(kds, uws)]
        for idx, (j, c) in enumerate(chains):
            qp = qs[idx].astype(F32) * jnp.exp(gc_cols[idx]) + a_uws[idx][:, dh:]
            mq_s[j, c, :dh, :] = k_uws[idx][:, dh:].astype(BF16)
            mq_s[j, c, dh:, :] = qp.astype(BF16)
            n_s[j, c] = k_uws[idx][:, :dh]
            oloc_s[j, pl.ds(rows0[idx], chunk), :] = a_uws[idx][:, :dh]

    def finalize(j, c):
        r0 = pl.multiple_of(c * chunk, chunk)
        o_ref[0, j, pl.ds(r0, chunk), :] = oloc_s[j, pl.ds(r0, chunk), :].astype(BF16)

    def scan_steps(grp_idx, states):
        for cc in range(cg):
            c = grp_idx * cg + cc
            r0 = pl.multiple_of(c * chunk, chunk)
            for j in range(hp):
                finalize(j, jnp.maximum(c - 1, 0))
            for j in range(hp):
                res = _dot(mq_s[j, c], states[j].astype(BF16))
                states[j] = states[j] * alast_s[j, pl.ds(c, 1), :] + res[:dh] + n_s[j, c]
                oloc_s[j, pl.ds(r0, chunk), :] += res[dh:]
            yield

    n_groups = n_chunks // cg
    for _ in prep_phases(0):
        pass

    def fused_body(g, carry):
        states = [state_s[j] for j in range(hp)]
        prep, scan = prep_phases(g + 1), scan_steps(g, states)
        live = True
        while live:
            live = next(prep, 0) is None
            next(scan, 0)
        for _ in scan:
            pass
        for j in range(hp):
            state_s[j] = states[j]
        return carry

    state_s[...] = jnp.zeros(state_s.shape, F32)
    lax.fori_loop(0, n_groups - 1, fused_body, 0)
    states = [state_s[j] for j in range(hp)]
    for _ in scan_steps(n_groups - 1, states):
        pass
    for j in range(hp):
        finalize(j, n_chunks - 1)


def _gdn(proj, abt, a_log, dt_bias, *, heads):
    bsz, _, s, dh = proj.shape
    hp, cg, chunk = GDN_HEADS_PER_STEP, GDN_CHUNKS_PER_ITER, GDN_CHUNK
    n_chunks = s // chunk
    ngrp = heads // hp
    abt4 = abt.reshape(bsz, 2 * heads, n_chunks, chunk)
    kern = functools.partial(_gdn_kernel, hp=hp, chunk=chunk, cg=cg)
    blk = (1, hp, s, dh)
    rowblk = (1, hp, n_chunks, chunk)
    smem = pl.BlockSpec(memory_space=pltpu.SMEM)
    return pl.pallas_call(
        kern,
        grid=(bsz, ngrp),
        in_specs=[
            pl.BlockSpec(blk, lambda b, g: (b, g, 0, 0)),
            pl.BlockSpec(blk, lambda b, g: (b, ngrp + g, 0, 0)),
            pl.BlockSpec(blk, lambda b, g: (b, 2 * ngrp + g, 0, 0)),
            pl.BlockSpec(rowblk, lambda b, g: (b, g, 0, 0)),
            pl.BlockSpec(rowblk, lambda b, g: (b, ngrp + g, 0, 0)),
            smem,
            smem,
        ],
        out_specs=pl.BlockSpec(blk, lambda b, g: (b, g, 0, 0)),
        out_shape=jax.ShapeDtypeStruct((bsz, heads, s, dh), BF16),
        scratch_shapes=[
            pltpu.VMEM((hp, n_chunks, chunk), F32),
            pltpu.VMEM((hp, n_chunks, chunk), F32),
            pltpu.VMEM((hp, n_chunks, dh), F32),
            pltpu.VMEM((hp, n_chunks, dh + chunk, dh), BF16),
            pltpu.VMEM((hp, n_chunks, dh, dh), F32),
            pltpu.VMEM((hp, s, dh), F32),
            pltpu.VMEM((hp, dh, dh), F32),
        ],
        compiler_params=pltpu.CompilerParams(
            dimension_semantics=("parallel", "parallel"), vmem_limit_bytes=VMEM_LIMIT),
        name="gdn",
    )(proj, proj, proj, abt4, abt4, a_log, dt_bias)


def kernel(x, c, ada_w, ada_b, ln_g, ln_b, moba_w_in, moba_w_out, gdn_w_in, gdn_conv_w, gdn_a_log,
           gdn_dt_bias, gdn_norm_w, gdn_w_out):
    bsz, s, d = x.shape
    depth = ada_w.shape[0]
    heads = d // HEAD_DIM
    width = heads * HEAD_DIM
    alpha = (2.0 * depth) ** 0.25

    mod = _modulation(c, ada_w, ada_b).reshape(depth, bsz, 3, 1, d)

    pending = None
    for i in range(depth):
        shift, scale, gate = mod[i, :, 0], mod[i, :, 1], mod[i, :, 2]
        j = i // 2
        if i % 2 == 0:
            res = _inproj(x, scale, shift, moba_w_in[j].astype(BF16), prev=pending, alpha=alpha,
                          lead_cols=width, lead_scale=float(HEAD_DIM) ** -0.5 * LOG2E)
            if pending is not None:
                x = res.pop(0)
            (proj,) = res
            o = _moba_attention(proj, heads=heads)
            pending = (o, proj, 3, gate, moba_w_out[j].astype(BF16), ln_g[i], ln_b[i], None)
        else:
            w_in = gdn_w_in[j]
            wab_t = w_in[:, 4 * width:].T.astype(BF16)
            res = _inproj(x, scale, shift, w_in[:, :4 * width].astype(BF16), wab_t, gdn_conv_w[j],
                          prev=pending, alpha=alpha)
            if pending is not None:
                x = res.pop(0)
            proj, abt = res
            o = _gdn(proj, abt, gdn_a_log[j], gdn_dt_bias[j], heads=heads)
            pending = (o, proj, 3, gate, gdn_w_out[j].astype(BF16), ln_g[i], ln_b[i], gdn_norm_w[j])
    o, zsrc, z_off, gate, w_out, g, b, norm_w = pending
    return _outproj_ln(o, x, gate, w_out, g, b, alpha=alpha, zsrc=zsrc, z_off=z_off, norm_w=norm_w)
```
